```python
import jax, jax.numpy as jnp
from jax import lax
import numpy as np

D_MODEL = 1024
BATCH = 8
SEQ = 2048
DEPTH = 4
DEC_BATCH = 128
DEC_SEQ = 4
PAST_LEN = 16384
PAGE_SIZE = 128

EPS = 1e-6
N_A = (DEPTH + 1) // 2
N_B = DEPTH // 2
CHUNK = 128
D_V = 2 * D_MODEL
H_A = 8
HD_A = D_V // H_A
POOL_WINDOWS = (2, 4, 8, 16)
N_POOL_GROUPS = len(POOL_WINDOWS)
GD_B = D_MODEL // N_POOL_GROUPS
POOL_BUF = max(POOL_WINDOWS) - 1
D_FF = 2816
CONV_W = 3

kernel_name = "hybrid_gmlp_pool_convffn_step"


def rmsnorm(x, g):
    xf = x.astype(jnp.float32)
    y = xf * lax.rsqrt(jnp.mean(xf * xf, axis=-1, keepdims=True) + EPS)
    return (y * g.astype(jnp.float32)).astype(x.dtype)


def gmlp_mixer(h, w_in, g_v, w_s, b_s, w_out):
    B, T, _ = h.shape
    z = jax.nn.gelu(jnp.einsum('btd,de->bte', h, w_in))
    u, v = jnp.split(z, 2, axis=-1)
    v = rmsnorm(v, g_v)
    n_chunks = -(-T // CHUNK)
    pad = n_chunks * CHUNK - T
    vp = jnp.pad(v, ((0, 0), (0, pad), (0, 0))).reshape(B, n_chunks, CHUNK, H_A, HD_A)
    mask = jnp.tril(jnp.ones((CHUNK, CHUNK), dtype=w_s.dtype))
    ws = w_s * mask[None]
    s = jnp.einsum('hij,bcjhe->bcihe', ws, vp) + jnp.transpose(b_s)[None, None, :, :, None]
    s = s.reshape(B, n_chunks * CHUNK, D_V)[:, :T]
    y = jnp.einsum('bte,ed->btd', u * s, w_out)
    return y, v


def pool_mixer(h, buf, start_pos, w_pool, scale):
    B, T, D = h.shape
    L = POOL_BUF
    hc = jnp.concatenate([buf.astype(h.dtype), h], axis=1).astype(jnp.float32)
    c = jnp.pad(jnp.cumsum(hc, axis=1), ((0, 0), (1, 0), (0, 0)))
    pos = start_pos + jnp.arange(T)
    end = c[:, L + 1:]
    means = []
    for g, w in enumerate(POOL_WINDOWS):
        sl = slice(g * GD_B, (g + 1) * GD_B)
        begin = c[:, L + 1 - w:L + 1 - w + T, sl]
        cnt = jnp.minimum(w, pos + 1).astype(jnp.float32)[None, :, None]
        means.append((end[..., sl] - begin) / cnt)
    mean = jnp.concatenate(means, axis=-1)
    p = (mean - hc[:, L:]).astype(h.dtype).reshape(B, T, N_POOL_GROUPS, GD_B)
    y = jnp.einsum('btgc,gce->btge', p, w_pool).reshape(B, T, D)
    return y * scale, hc[:, -L:].astype(h.dtype)


def conv_ffn(h, buf, w_gate, w_val, conv_w, conv_b, w_down):
    T = h.shape[1]
    a = jnp.einsum('btd,df->btf', h, w_gate)
    val = jnp.einsum('btd,df->btf', h, w_val)
    ac = jnp.concatenate([buf.astype(a.dtype), a], axis=1)
    conv = conv_b + conv_w[0] * ac[:, 0:T]
    for k in range(1, CONV_W):
        conv = conv + conv_w[k] * ac[:, k:k + T]
    y = jnp.einsum('btf,fd->btd', jax.nn.silu(conv) * val, w_down)
    return y, ac[:, -(CONV_W - 1):]


def trunk(x, start_pos, pool_bufs, conv_bufs, norm_mix, norm_ffn, norm_final,
          w_in_a, g_v_a, w_s_a, b_s_a, w_out_a, w_pool_b, scale_b,
          w_gate, w_val, conv_w, conv_b, w_down):
    v_rows, pool_new, conv_new = [], [], []
    for i in range(DEPTH):
        h = rmsnorm(x, norm_mix[i])
        j = i // 2
        if i % 2 == 0:
            y, v = gmlp_mixer(h, w_in_a[j], g_v_a[j], w_s_a[j], b_s_a[j], w_out_a[j])
            v_rows.append(v)
        else:
            y, pb = pool_mixer(h, pool_bufs[j], start_pos, w_pool_b[j], scale_b[j])
            pool_new.append(pb)
        x = x + y
        h = rmsnorm(x, norm_ffn[i])
        y, cb = conv_ffn(h, conv_bufs[i], w_gate[i], w_val[i], conv_w[i], conv_b[i], w_down[i])
        conv_new.append(cb)
        x = x + y
    return rmsnorm(x, norm_final), jnp.stack(v_rows), jnp.stack(pool_new), jnp.stack(conv_new)


def setup_inputs(seed: int = 0) -> dict:
    key = jax.random.key(seed)
    ks = jax.random.split(key, 20)
    f32 = jnp.float32
    nrm = lambda k, shape, s: jax.random.normal(k, shape, f32) * s
    return {
        "x_prompt": nrm(ks[0], (BATCH, SEQ, D_MODEL), 1.0),
        "x_sample": nrm(ks[1], (DEC_BATCH, DEC_SEQ, D_MODEL), 1.0),
        "state_pool": nrm(ks[2], (N_B, DEC_BATCH, POOL_BUF, D_MODEL), 1.0),
        "state_ffn_conv": nrm(ks[3], (DEPTH, DEC_BATCH, CONV_W - 1, D_FF), 0.5),
        "norm_mix": 1.0 + nrm(ks[4], (DEPTH, D_MODEL), 0.05),
        "norm_ffn": 1.0 + nrm(ks[5], (DEPTH, D_MODEL), 0.05),
        "norm_final": 1.0 + nrm(ks[6], (D_MODEL,), 0.05),
        "w_in_a": nrm(ks[7], (N_A, D_MODEL, 2 * D_V), D_MODEL ** -0.5),
        "g_v_a": 1.0 + nrm(ks[8], (N_A, D_V), 0.05),
        "w_s_a": nrm(ks[9], (N_A, H_A, CHUNK, CHUNK), CHUNK ** -0.5),
        "b_s_a": 1.0 + nrm(ks[10], (N_A, H_A, CHUNK), 0.05),
        "w_out_a": nrm(ks[11], (N_A, D_V, D_MODEL), D_V ** -0.5),
        "w_pool_b": nrm(ks[12], (N_B, N_POOL_GROUPS, GD_B, GD_B), GD_B ** -0.5),
        "scale_b": 1.0 + nrm(ks[13], (N_B, D_MODEL), 0.1),
        "w_gate": nrm(ks[14], (DEPTH, D_MODEL, D_FF), D_MODEL ** -0.5),
        "w_val": nrm(ks[15], (DEPTH, D_MODEL, D_FF), D_MODEL ** -0.5),
        "conv_w": nrm(ks[16], (DEPTH, CONV_W, D_FF), CONV_W ** -0.5),
        "conv_b": nrm(ks[17], (DEPTH, D_FF), 0.01),
        "w_down": nrm(ks[18], (DEPTH, D_FF, D_MODEL), D_FF ** -0.5),
    }


def reference(x_prompt, x_sample, state_pool, state_ffn_conv, norm_mix, norm_ffn, norm_final,
              w_in_a, g_v_a, w_s_a, b_s_a, w_out_a, w_pool_b, scale_b,
              w_gate, w_val, conv_w, conv_b, w_down):
    weights = (norm_mix, norm_ffn, norm_final, w_in_a, g_v_a, w_s_a, b_s_a, w_out_a,
               w_pool_b, scale_b, w_gate, w_val, conv_w, conv_b, w_down)
    pool0 = jnp.zeros((N_B, BATCH, POOL_BUF, D_MODEL), x_prompt.dtype)
    conv0 = jnp.zeros((DEPTH, BATCH, CONV_W - 1, D_FF), x_prompt.dtype)
    y_prompt, _, pool_prompt, ffn_conv_prompt = trunk(x_prompt, 0, pool0, conv0, *weights)
    y_sample, gmlp_v_sample, pool_sample, ffn_conv_sample = trunk(
        x_sample, PAST_LEN, state_pool, state_ffn_conv, *weights)
    return (y_prompt, y_sample, gmlp_v_sample, pool_prompt, pool_sample, ffn_conv_prompt, ffn_conv_sample)
```

```python
import functools
import math

import jax
import jax.numpy as jnp
from jax import lax
from jax.experimental import pallas as pl
from jax.experimental.pallas import tpu as pltpu

EPS = 1e-6
CHUNK = 128
POOL_WINDOWS = (2, 4, 8, 16)
POOL_BUF = max(POOL_WINDOWS) - 1
CONV_W = 3
SUBLANES = 8
TOKEN_TILE = 512
FF_CHUNK = 512
VMEM_LIMIT_BYTES = 56 * 1024 * 1024

_GELU_C = math.sqrt(2.0 / math.pi)


def _const_spec(shape):
    zeros = (0,) * len(shape)
    return pl.BlockSpec(shape, lambda i: zeros, pipeline_mode=pl.Buffered(1))


def _rms(x, g):
    ms = jnp.mean(x * x, axis=-1, keepdims=True)
    return x * lax.rsqrt(ms + EPS) * g


def _gelu_tanh(x):
    inner = x * (_GELU_C + (_GELU_C * 0.044715) * (x * x))
    return x * (0.5 + 0.5 * jnp.tanh(inner))


def _silu(x):
    return x * (1.0 / (1.0 + jnp.exp(-x)))


def _compiler_params():
    return pltpu.CompilerParams(dimension_semantics=("arbitrary",),
                                vmem_limit_bytes=VMEM_LIMIT_BYTES)


def _gmlp_kernel(x_ref, nm_ref, win_ref, gv_ref, ws_ref, bias_ref, wout_ref, *rest,
                 tm, d_v, n_heads, emit_v):
    if emit_v:
        o_ref, v_ref, m_ref = rest
    else:
        o_ref, m_ref = rest
    hd = d_v // n_heads
    x = x_ref[...]
    h = _rms(x, nm_ref[...]).astype(jnp.bfloat16)
    z = _gelu_tanh(jnp.dot(h, win_ref[...], preferred_element_type=jnp.float32))
    u = z[:, :d_v]
    v = _rms(z[:, d_v:], gv_ref[...])
    if emit_v:
        v_ref[...] = v
    vb = v.astype(jnp.bfloat16)
    row = lax.broadcasted_iota(jnp.int32, (CHUNK, CHUNK), 0)
    col = lax.broadcasted_iota(jnp.int32, (CHUNK, CHUNK), 1)
    causal = row >= col
    for hh in range(n_heads):
        ws = jnp.where(causal, ws_ref[hh], 0.0).astype(jnp.bfloat16)
        c0, c1 = hh * hd, (hh + 1) * hd
        for c in range(tm // CHUNK):
            r0, r1 = c * CHUNK, (c + 1) * CHUNK
            s = jnp.dot(ws, vb[r0:r1, c0:c1], preferred_element_type=jnp.float32)
            s = s + bias_ref[:, c0:c1]
            m_ref[r0:r1, c0:c1] = (u[r0:r1, c0:c1] * s).astype(jnp.bfloat16)
    y = jnp.dot(m_ref[...], wout_ref[...], preferred_element_type=jnp.float32)
    o_ref[...] = x + y


def _gmlp_call(x2d, nm, win, gv, ws, bias, wout, *, emit_v):
    n, d = x2d.shape
    d_v = wout.shape[0]
    n_heads = ws.shape[0]
    tm = min(TOKEN_TILE, n)
    assert n % tm == 0 and tm % CHUNK == 0
    out_shape = [jax.ShapeDtypeStruct((n, d), jnp.float32)]
    out_specs = [pl.BlockSpec((tm, d), lambda i: (i, 0))]
    if emit_v:
        out_shape.append(jax.ShapeDtypeStruct((n, d_v), jnp.float32))
        out_specs.append(pl.BlockSpec((tm, d_v), lambda i: (i, 0)))
    res = pl.pallas_call(
        functools.partial(_gmlp_kernel, tm=tm, d_v=d_v, n_heads=n_heads, emit_v=emit_v),
        grid=(n // tm,),
        in_specs=[
            pl.BlockSpec((tm, d), lambda i: (i, 0)),
            _const_spec((1, d)),
            _const_spec(win.shape),
            _const_spec((1, d_v)),
            _const_spec(ws.shape),
            _const_spec(bias.shape),
            _const_spec(wout.shape),
        ],
        out_specs=out_specs,
        out_shape=out_shape,
        scratch_shapes=[pltpu.VMEM((tm, d_v), jnp.bfloat16)],
        compiler_params=_compiler_params(),
        name="gmlp_mixer",
    )(x2d, nm.reshape(1, d), win, gv.reshape(1, d_v), ws, bias, wout)
    return res if emit_v else res[0]


def _ffn_kernel(*refs, tm, d_ff, tiles_per_seq, sample, final):
    it = iter(refs)
    x_ref, nf_ref, wg_ref, wv_ref, cw_ref, cb_ref, wd_ref = (next(it) for _ in range(7))
    e1_ref = e2_ref = ng_ref = None
    if sample:
        e1_ref, e2_ref = next(it), next(it)
    if final:
        ng_ref = next(it)
    o_ref, a_out_ref = next(it), next(it)
    g_ref = next(it)
    carry_ref = None if sample else next(it)

    x = x_ref[...]
    h = _rms(x, nf_ref[...]).astype(jnp.bfloat16)
    first = None if sample else (pl.program_id(0) % tiles_per_seq == 0)
    bounds = list(range(0, d_ff, FF_CHUNK)) + [d_ff]
    for c0, c1 in zip(bounds[:-1], bounds[1:]):
        fc = c1 - c0
        a = jnp.dot(h, wg_ref[:, c0:c1], preferred_element_type=jnp.float32)
        val = jnp.dot(h, wv_ref[:, c0:c1], preferred_element_type=jnp.float32)
        if sample:
            head = jnp.zeros((SUBLANES, fc), jnp.float32)
        else:
            head = jnp.where(first, 0.0, carry_ref[:, c0:c1])
            carry_ref[:, c0:c1] = a[tm - SUBLANES:, :]
        ext = jnp.concatenate([head, a], axis=0)
        prev1 = ext[SUBLANES - 1:SUBLANES - 1 + tm]
        prev2 = ext[SUBLANES - 2:SUBLANES - 2 + tm]
        if sample:
            t_idx = lax.broadcasted_iota(jnp.int32, (tm, fc), 0) % 4
            prev1 = jnp.where(t_idx == 0, e1_ref[:, c0:c1], prev1)
            prev2 = jnp.where(t_idx < 2, e2_ref[:, c0:c1], prev2)
            a_out_ref[:, c0:c1] = a
        else:
            a_out_ref[0, :, c0:c1] = a[tm - (CONV_W - 1):, :]
        conv = (cb_ref[:, c0:c1] + cw_ref[0:1, c0:c1] * prev2
                + cw_ref[1:2, c0:c1] * prev1 + cw_ref[2:3, c0:c1] * a)
        g_ref[:, c0:c1] = (_silu(conv) * val).astype(jnp.bfloat16)
    y = jnp.dot(g_ref[...], wd_ref[...], preferred_element_type=jnp.float32)
    out = x + y
    if final:
        out = _rms(out, ng_ref[...])
    o_ref[...] = out


def _ffn_call(x2d, nf, wg, wv, cw, cb, wd, *, seq_len, state=None, norm_final=None):
    n, d = x2d.shape
    d_ff = wd.shape[0]
    sample = state is not None
    final = norm_final is not None
    tm = min(TOKEN_TILE, n)
    assert n % tm == 0
    if sample:
        assert n == tm and seq_len == 4
        tiles_per_seq = 1
    else:
        assert seq_len % tm == 0
        tiles_per_seq = seq_len // tm
    n_seq = n // seq_len
    args = [x2d, nf.reshape(1, d), wg, wv, cw, cb.reshape(1, d_ff), wd]
    in_specs = [
        pl.BlockSpec((tm, d), lambda i: (i, 0)),
        _const_spec((1, d)),
        _const_spec(wg.shape),
        _const_spec(wv.shape),
        _const_spec(cw.shape),
        _const_spec((1, d_ff)),
        _const_spec(wd.shape),
    ]
    if sample:
        args += list(state)
        in_specs += [_const_spec((tm, d_ff)), _const_spec((tm, d_ff))]
    if final:
        args.append(norm_final.reshape(1, d))
        in_specs.append(_const_spec((1, d)))
    if sample:
        a_shape = jax.ShapeDtypeStruct((n, d_ff), jnp.float32)
        a_spec = pl.BlockSpec((tm, d_ff), lambda i: (i, 0))
    else:
        a_shape = jax.ShapeDtypeStruct((n_seq, CONV_W - 1, d_ff), jnp.float32)
        a_spec = pl.BlockSpec((1, CONV_W - 1, d_ff), lambda i: (i // tiles_per_seq, 0, 0))
    scratch = [pltpu.VMEM((tm, d_ff), jnp.bfloat16)]
    if not sample:
        scratch.append(pltpu.VMEM((SUBLANES, d_ff), jnp.float32))
    return pl.pallas_call(
        functools.partial(_ffn_kernel, tm=tm, d_ff=d_ff, tiles_per_seq=tiles_per_seq,
                          sample=sample, final=final),
        grid=(n // tm,),
        in_specs=in_specs,
        out_specs=[pl.BlockSpec((tm, d), lambda i: (i, 0)), a_spec],
        out_shape=[jax.ShapeDtypeStruct((n, d), jnp.float32), a_shape],
        scratch_shapes=scratch,
        compiler_params=_compiler_params(),
        name="conv_ffn_sample" if sample else "conv_ffn",
    )(*args)


def _pool_project(p_groups, wp_ref, scale_ref):
    gd = wp_ref.shape[1]
    ys = [jnp.dot(p.astype(jnp.bfloat16), wp_ref[g], preferred_element_type=jnp.float32)
          * scale_ref[:, g * gd:(g + 1) * gd] for g, p in enumerate(p_groups)]
    return jnp.concatenate(ys, axis=-1)


def _pool_kernel(x_ref, nm_ref, wp_ref, scale_ref, o_ref, tail_ref, carry_ref, *, tm, tiles_per_seq):
    halo = POOL_BUF + 1
    gd = wp_ref.shape[1]
    x = x_ref[...]
    h = _rms(x, nm_ref[...])
    tile_in_seq = pl.program_id(0) % tiles_per_seq
    prev = jnp.where(tile_in_seq == 0, 0.0, carry_ref[...])
    carry_ref[...] = h[tm - halo:, :]
    tail_ref[0] = h[tm - POOL_BUF:, :]
    pos = tile_in_seq * tm + lax.broadcasted_iota(jnp.int32, (tm, gd), 0)
    p_groups = []
    for g, w in enumerate(POOL_WINDOWS):
        c0, c1 = g * gd, (g + 1) * gd
        hg = h[:, c0:c1]
        acc = jnp.concatenate([prev[:, c0:c1], hg], axis=0)
        span = 1
        while span < w:
            acc = acc[span:] + acc[:-span]
            span *= 2
        total = acc[halo - (w - 1):]
        cnt = jnp.minimum(w, pos + 1).astype(jnp.float32)
        p_groups.append(total / cnt - hg)
    o_ref[...] = x + _pool_project(p_groups, wp_ref, scale_ref)


def _pool_call(x2d, nm, wp, scale, *, seq_len):
    n, d = x2d.shape
    tm = min(TOKEN_TILE, n)
    assert n % tm == 0 and seq_len % tm == 0
    tiles_per_seq = seq_len // tm
    n_seq = n // seq_len
    return pl.pallas_call(
        functools.partial(_pool_kernel, tm=tm, tiles_per_seq=tiles_per_seq),
        grid=(n // tm,),
        in_specs=[
            pl.BlockSpec((tm, d), lambda i: (i, 0)),
            _const_spec((1, d)),
            _const_spec(wp.shape),
            _const_spec((1, d)),
        ],
        out_specs=[pl.BlockSpec((tm, d), lambda i: (i, 0)),
                   pl.BlockSpec((1, POOL_BUF, d), lambda i: (i // tiles_per_seq, 0, 0))],
        out_shape=[jax.ShapeDtypeStruct((n, d), jnp.float32),
                   jax.ShapeDtypeStruct((n_seq, POOL_BUF, d), jnp.float32)],
        scratch_shapes=[pltpu.VMEM((POOL_BUF + 1, d), jnp.float32)],
        compiler_params=_compiler_params(),
        name="pool_mixer",
    )(x2d, nm.reshape(1, d), wp, scale.reshape(1, d))


def _pool_sample_kernel(x_ref, nm_ref, buf_ref, wp_ref, scale_ref, o_ref, h_ref):
    gd = wp_ref.shape[1]
    seq_len = x_ref.shape[0]
    for t in range(seq_len):
        h_ref[t] = _rms(x_ref[t], nm_ref[...])

    def slab(k, c0, c1):
        return buf_ref[k, :, c0:c1] if k < POOL_BUF else h_ref[k - POOL_BUF, :, c0:c1]

    for t in range(seq_len):
        last = POOL_BUF + t
        p_groups = []
        for g, w in enumerate(POOL_WINDOWS):
            c0, c1 = g * gd, (g + 1) * gd
            total = slab(last, c0, c1)
            for k in range(last - w + 1, last):
                total = total + slab(k, c0, c1)
            p_groups.append(total / float(w) - slab(last, c0, c1))
        o_ref[t] = x_ref[t] + _pool_project(p_groups, wp_ref, scale_ref)


def _pool_sample_call(x_tm, nm, buf_tm, wp, scale):
    seq_len, n_seq, d = x_tm.shape
    out = jax.ShapeDtypeStruct(x_tm.shape, jnp.float32)
    return pl.pallas_call(
        _pool_sample_kernel,
        grid=(1,),
        in_specs=[
            _const_spec(x_tm.shape),
            _const_spec((1, d)),
            _const_spec(buf_tm.shape),
            _const_spec(wp.shape),
            _const_spec((1, d)),
        ],
        out_specs=[pl.BlockSpec(x_tm.shape, lambda i: (0, 0, 0)),
                   pl.BlockSpec(x_tm.shape, lambda i: (0, 0, 0))],
        out_shape=[out, out],
        compiler_params=_compiler_params(),
        name="pool_mixer_sample",
    )(x_tm, nm.reshape(1, d), buf_tm, wp, scale.reshape(1, d))


def kernel(x_prompt, x_sample, state_pool, state_ffn_conv, norm_mix, norm_ffn, norm_final,
           w_in_a, g_v_a, w_s_a, b_s_a, w_out_a, w_pool_b, scale_b,
           w_gate, w_val, conv_w, conv_b, w_down):
    batch, seq, d = x_prompt.shape
    dec_batch, dec_seq, _ = x_sample.shape
    depth = w_gate.shape[0]
    n_heads = w_s_a.shape[1]
    d_v = w_out_a.shape[1]
    d_ff = w_down.shape[1]
    hd = d_v // n_heads
    bf16 = jnp.bfloat16
    assert dec_seq == 4 and CHUNK % dec_seq == 0 and dec_seq >= CONV_W - 1

    w_in_b, w_out_b = w_in_a.astype(bf16), w_out_a.astype(bf16)
    w_pool_bf = w_pool_b.astype(bf16)
    w_gate_b, w_val_b, w_down_b = w_gate.astype(bf16), w_val.astype(bf16), w_down.astype(bf16)

    reps = CHUNK // dec_seq
    bias_p = jnp.repeat(jnp.swapaxes(b_s_a, 1, 2), hd, axis=2)
    ws_s = jax.vmap(jax.vmap(lambda m: jnp.kron(jnp.eye(reps, dtype=m.dtype), m)))(
        w_s_a[:, :, :dec_seq, :dec_seq])
    bias_s = jnp.tile(bias_p[:, :dec_seq, :], (1, reps, 1))

    xp = x_prompt.reshape(batch * seq, d)
    xs = x_sample.reshape(dec_batch * dec_seq, d)
    v_rows, pool_p, pool_s, conv_p, conv_s = [], [], [], [], []
    for i in range(depth):
        j = i // 2
        if i % 2 == 0:
            xp = _gmlp_call(xp, norm_mix[i], w_in_b[j], g_v_a[j], w_s_a[j], bias_p[j], w_out_b[j],
                            emit_v=False)
            xs, v = _gmlp_call(xs, norm_mix[i], w_in_b[j], g_v_a[j], ws_s[j], bias_s[j], w_out_b[j],
                               emit_v=True)
            v_rows.append(v.reshape(dec_batch, dec_seq, d_v))
        else:
            xp, tail = _pool_call(xp, norm_mix[i], w_pool_bf[j], scale_b[j], seq_len=seq)
            pool_p.append(tail)
            xs_tm = jnp.swapaxes(xs.reshape(dec_batch, dec_seq, d), 0, 1)
            xs_tm, hs_tm = _pool_sample_call(xs_tm, norm_mix[i], jnp.swapaxes(state_pool[j], 0, 1),
                                             w_pool_bf[j], scale_b[j])
            xs = jnp.swapaxes(xs_tm, 0, 1).reshape(dec_batch * dec_seq, d)
            pool_s.append(jnp.concatenate(
                [state_pool[j][:, dec_seq:], jnp.swapaxes(hs_tm, 0, 1)], axis=1))
        nfin = norm_final if i == depth - 1 else None
        xp, a_tail = _ffn_call(xp, norm_ffn[i], w_gate_b[i], w_val_b[i], conv_w[i], conv_b[i],
                               w_down_b[i], seq_len=seq, norm_final=nfin)
        conv_p.append(a_tail)
        buf = state_ffn_conv[i]
        zero = jnp.zeros_like(buf[:, 0])
        e1 = jnp.stack([buf[:, 1], zero, zero, zero], axis=1).reshape(dec_batch * dec_seq, d_ff)
        e2 = jnp.stack([buf[:, 0], buf[:, 1], zero, zero], axis=1).reshape(dec_batch * dec_seq, d_ff)
        xs, a_s = _ffn_call(xs, norm_ffn[i], w_gate_b[i], w_val_b[i], conv_w[i], conv_b[i],
                            w_down_b[i], seq_len=dec_seq, state=(e1, e2), norm_final=nfin)
        conv_s.append(a_s.reshape(dec_batch, dec_seq, d_ff)[:, dec_seq - (CONV_W - 1):])
    return (xp.reshape(batch, seq, d), xs.reshape(dec_batch, dec_seq, d), jnp.stack(v_rows),
            jnp.stack(pool_p), jnp.stack(pool_s), jnp.stack(conv_p), jnp.stack(conv_s))
```

```python
import functools
import math

import jax
import jax.numpy as jnp
from jax import lax
from jax.experimental import pallas as pl
from jax.experimental.pallas import tpu as pltpu

EPS = 1e-6
CHUNK = 128
POOL_WINDOWS = (2, 4, 8, 16)
POOL_BUF = max(POOL_WINDOWS) - 1
CONV_W = 3
SUBLANES = 8
TOKEN_TILE = 512
FF_CHUNK = 512
VMEM_LIMIT_BYTES = 56 * 1024 * 1024

_GELU_C = math.sqrt(2.0 / math.pi)


def _const_spec(shape):
    zeros = (0,) * len(shape)
    return pl.BlockSpec(shape, lambda i: zeros, pipeline_mode=pl.Buffered(1))


def _layer_spec(stacked_shape, layer):
    tail = (0,) * (len(stacked_shape) - 1)
    return pl.BlockSpec((None,) + tuple(stacked_shape[1:]), lambda i: (layer,) + tail,
                        pipeline_mode=pl.Buffered(1))


def _rms(x, g):
    ms = jnp.mean(x * x, axis=-1, keepdims=True)
    return x * lax.rsqrt(ms + EPS) * g


def _gelu_tanh(x):
    inner = x * (_GELU_C + (_GELU_C * 0.044715) * (x * x))
    return x * (0.5 + 0.5 * jnp.tanh(inner))


def _silu(x):
    return x * (1.0 / (1.0 + jnp.exp(-x)))


def _compiler_params():
    return pltpu.CompilerParams(dimension_semantics=("arbitrary",),
                                vmem_limit_bytes=VMEM_LIMIT_BYTES)


def _load_slabs(ref3d):
    return jnp.concatenate([ref3d[:, t, :] for t in range(ref3d.shape[1])], axis=0)


def _store_slabs(ref3d, val):
    n_seq = ref3d.shape[0]
    for t in range(ref3d.shape[1]):
        ref3d[:, t, :] = val[t * n_seq:(t + 1) * n_seq]


def _gmlp_tile(x, nm, win_ref, gv, wout_ref, m_ref, gate_fn):
    rows = x.shape[0]
    d_v = wout_ref.shape[0]
    h = _rms(x, nm).astype(jnp.bfloat16)
    z = _gelu_tanh(jnp.dot(h, win_ref[...], preferred_element_type=jnp.float32))
    u = z[:, :d_v]
    v = _rms(z[:, d_v:], gv)
    gate_fn(u, v)
    y = jnp.dot(m_ref[:rows, :], wout_ref[...], preferred_element_type=jnp.float32)
    return x + y, v


def _gmlp_kernel(xp_ref, xs_ref, nm_ref, win_ref, gv_ref, ws_ref, bias_ref, ws4_ref, b4_ref, wout_ref,
                 op_ref, os_ref, vs_ref, m_ref, *, layer, mixer, n_prompt_tiles):
    i = pl.program_id(0)
    n_heads = ws_ref.shape[0]
    d_v = wout_ref.shape[0]
    hd = d_v // n_heads
    nm = nm_ref[layer:layer + 1, :]
    gv = gv_ref[mixer:mixer + 1, :]

    @pl.when(i < n_prompt_tiles)
    def _prompt():
        tm = xp_ref.shape[0]
        row = lax.broadcasted_iota(jnp.int32, (CHUNK, CHUNK), 0)
        col = lax.broadcasted_iota(jnp.int32, (CHUNK, CHUNK), 1)

        def gate(u, v):
            vb = v.astype(jnp.bfloat16)
            for hh in range(n_heads):
                ws = jnp.where(row >= col, ws_ref[hh], 0.0).astype(jnp.bfloat16)
                c0, c1 = hh * hd, (hh + 1) * hd
                for c in range(tm // CHUNK):
                    r0, r1 = c * CHUNK, (c + 1) * CHUNK
                    s = jnp.dot(ws, vb[r0:r1, c0:c1], preferred_element_type=jnp.float32)
                    s = s + bias_ref[:, c0:c1]
                    m_ref[r0:r1, c0:c1] = (u[r0:r1, c0:c1] * s).astype(jnp.bfloat16)

        out, _ = _gmlp_tile(xp_ref[...], nm, win_ref, gv, wout_ref, m_ref, gate)
        op_ref[...] = out

    @pl.when(i == n_prompt_tiles)
    def _sample():
        n_seq, seq_len, _ = xs_ref.shape

        def gate(u, v):
            for hh in range(n_heads):
                c0, c1 = hh * hd, (hh + 1) * hd
                vj = [v[j * n_seq:(j + 1) * n_seq, c0:c1] for j in range(seq_len)]
                for t in range(seq_len):
                    s = b4_ref[hh, t] + ws4_ref[hh, t * seq_len] * vj[0]
                    for j in range(1, t + 1):
                        s = s + ws4_ref[hh, t * seq_len + j] * vj[j]
                    r0, r1 = t * n_seq, (t + 1) * n_seq
                    m_ref[r0:r1, c0:c1] = (u[r0:r1, c0:c1] * s).astype(jnp.bfloat16)

        out, v = _gmlp_tile(_load_slabs(xs_ref), nm, win_ref, gv, wout_ref, m_ref, gate)
        _store_slabs(os_ref, out)
        _store_slabs(vs_ref, v)


def _gmlp_call(xp, xs, norm_mix, w_in, g_v, w_s, bias, ws4, b4, w_out, *, layer, mixer):
    n, d = xp.shape
    n_seq, seq_len, _ = xs.shape
    d_v = w_out.shape[1]
    tm = TOKEN_TILE
    assert n % tm == 0 and tm % CHUNK == 0 and n_seq % SUBLANES == 0
    n_tiles = n // tm
    m_rows = max(tm, n_seq * seq_len)
    smem = pl.BlockSpec(memory_space=pltpu.SMEM)
    prompt_map = lambda i: (jnp.minimum(i, n_tiles - 1), 0)
    return pl.pallas_call(
        functools.partial(_gmlp_kernel, layer=layer, mixer=mixer, n_prompt_tiles=n_tiles),
        grid=(n_tiles + 1,),
        in_specs=[
            pl.BlockSpec((tm, d), prompt_map),
            _const_spec(xs.shape),
            _const_spec(norm_mix.shape),
            _layer_spec(w_in.shape, mixer),
            _const_spec(g_v.shape),
            _layer_spec(w_s.shape, mixer),
            _layer_spec(bias.shape, mixer),
            smem, smem,
            _layer_spec(w_out.shape, mixer),
        ],
        out_specs=[pl.BlockSpec((tm, d), prompt_map),
                   pl.BlockSpec(xs.shape, lambda i: (0, 0, 0)),
                   pl.BlockSpec((n_seq, seq_len, d_v), lambda i: (0, 0, 0))],
        out_shape=[jax.ShapeDtypeStruct((n, d), jnp.float32),
                   jax.ShapeDtypeStruct(xs.shape, jnp.float32),
                   jax.ShapeDtypeStruct((n_seq, seq_len, d_v), jnp.float32)],
        scratch_shapes=[pltpu.VMEM((m_rows, d_v), jnp.bfloat16)],
        compiler_params=_compiler_params(),
        name="gmlp_mixer",
    )(xp, xs, norm_mix, w_in, g_v, w_s, bias, ws4, b4, w_out)


def _ffn_tile(x, nf, wg_ref, wv_ref, cw_ref, cb_ref, wd_ref, g_ref, history_fn, ng):
    rows = x.shape[0]
    d_ff = wd_ref.shape[0]
    h = _rms(x, nf).astype(jnp.bfloat16)
    bounds = list(range(0, d_ff, FF_CHUNK)) + [d_ff]
    for c0, c1 in zip(bounds[:-1], bounds[1:]):
        a = jnp.dot(h, wg_ref[:, c0:c1], preferred_element_type=jnp.float32)
        val = jnp.dot(h, wv_ref[:, c0:c1], preferred_element_type=jnp.float32)
        prev1, prev2 = history_fn(a, c0, c1)
        conv = (cb_ref[:, c0:c1] + cw_ref[0:1, c0:c1] * prev2
                + cw_ref[1:2, c0:c1] * prev1 + cw_ref[2:3, c0:c1] * a)
        g_ref[:rows, c0:c1] = (_silu(conv) * val).astype(jnp.bfloat16)
    y = jnp.dot(g_ref[:rows, :], wd_ref[...], preferred_element_type=jnp.float32)
    out = x + y
    return out if ng is None else _rms(out, ng)


def _ffn_kernel(*refs, layer, n_prompt_tiles, tiles_per_seq, final):
    it = iter(refs)
    xp_ref, xs_ref, cs_ref, nf_ref, wg_ref, wv_ref, cw_ref, cbias_ref, wd_ref = (next(it) for _ in range(9))
    ng_ref = next(it) if final else None
    op_ref, os_ref, cp_ref, cso_ref, g_ref, carry_ref = (next(it) for _ in range(6))
    i = pl.program_id(0)
    nf = nf_ref[layer:layer + 1, :]
    cb_ref = cbias_ref.at[layer:layer + 1, :]
    ng = None if ng_ref is None else ng_ref[...]
    keep = CONV_W - 1

    @pl.when(i < n_prompt_tiles)
    def _prompt():
        tm = xp_ref.shape[0]
        first = i % tiles_per_seq == 0

        def history(a, c0, c1):
            head = jnp.where(first, 0.0, carry_ref[:, c0:c1])
            carry_ref[:, c0:c1] = a[tm - SUBLANES:, :]
            cp_ref[0, :, c0:c1] = a[tm - keep:, :]
            ext = jnp.concatenate([head, a], axis=0)
            return ext[SUBLANES - 1:SUBLANES - 1 + tm], ext[SUBLANES - 2:SUBLANES - 2 + tm]

        op_ref[...] = _ffn_tile(xp_ref[...], nf, wg_ref, wv_ref, cw_ref, cb_ref, wd_ref, g_ref, history, ng)

    @pl.when(i == n_prompt_tiles)
    def _sample():
        n_seq, seq_len, _ = xs_ref.shape
        rows = n_seq * seq_len

        def history(a, c0, c1):
            old = [cs_ref[:, k, c0:c1] for k in range(keep)]
            for k in range(keep):
                t = seq_len - keep + k
                cso_ref[:, k, c0:c1] = a[t * n_seq:(t + 1) * n_seq]
            prev1 = jnp.concatenate([old[1], a[:rows - n_seq]], axis=0)
            prev2 = jnp.concatenate([old[0], old[1], a[:rows - 2 * n_seq]], axis=0)
            return prev1, prev2

        out = _ffn_tile(_load_slabs(xs_ref), nf, wg_ref, wv_ref, cw_ref, cb_ref, wd_ref, g_ref, history, ng)
        _store_slabs(os_ref, out)


def _ffn_call(xp, xs, conv_state, norm_ffn, wg, wv, cw, cb, wd, norm_final, *, layer, seq):
    n, d = xp.shape
    n_seq, seq_len, _ = xs.shape
    d_ff = wd.shape[1]
    tm = TOKEN_TILE
    assert n % tm == 0 and seq % tm == 0 and seq_len >= CONV_W - 1 and n_seq % SUBLANES == 0
    n_tiles, tiles_per_seq = n // tm, seq // tm
    final = norm_final is not None
    prompt_map = lambda i: (jnp.minimum(i, n_tiles - 1), 0)
    args = [xp, xs, conv_state, norm_ffn, wg, wv, cw, cb, wd]
    in_specs = [
        pl.BlockSpec((tm, d), prompt_map),
        _const_spec(xs.shape),
        _layer_spec(conv_state.shape, layer),
        _const_spec(norm_ffn.shape),
        _layer_spec(wg.shape, layer),
        _layer_spec(wv.shape, layer),
        _layer_spec(cw.shape, layer),
        _const_spec(cb.shape),
        _layer_spec(wd.shape, layer),
    ]
    if final:
        args.append(norm_final.reshape(1, d))
        in_specs.append(_const_spec((1, d)))
    keep = CONV_W - 1
    return pl.pallas_call(
        functools.partial(_ffn_kernel, layer=layer, n_prompt_tiles=n_tiles,
                          tiles_per_seq=tiles_per_seq, final=final),
        grid=(n_tiles + 1,),
        in_specs=in_specs,
        out_specs=[pl.BlockSpec((tm, d), prompt_map),
                   pl.BlockSpec(xs.shape, lambda i: (0, 0, 0)),
                   pl.BlockSpec((1, keep, d_ff),
                                lambda i: (jnp.minimum(i, n_tiles - 1) // tiles_per_seq, 0, 0)),
                   pl.BlockSpec((n_seq, keep, d_ff), lambda i: (0, 0, 0))],
        out_shape=[jax.ShapeDtypeStruct((n, d), jnp.float32),
                   jax.ShapeDtypeStruct(xs.shape, jnp.float32),
                   jax.ShapeDtypeStruct((n // seq, keep, d_ff), jnp.float32),
                   jax.ShapeDtypeStruct((n_seq, keep, d_ff), jnp.float32)],
        scratch_shapes=[pltpu.VMEM((max(tm, n_seq * seq_len), d_ff), jnp.bfloat16),
                        pltpu.VMEM((SUBLANES, d_ff), jnp.float32)],
        compiler_params=_compiler_params(),
        name="conv_ffn",
    )(*args)


def _pool_project(p_groups, wp_ref, scale):
    gd = wp_ref.shape[1]
    ys = [jnp.dot(p.astype(jnp.bfloat16), wp_ref[g], preferred_element_type=jnp.float32)
          * scale[:, g * gd:(g + 1) * gd] for g, p in enumerate(p_groups)]
    return jnp.concatenate(ys, axis=-1)


def _pool_kernel(xp_ref, xs_ref, ps_ref, nm_ref, wp_ref, scale_ref, op_ref, os_ref, pp_ref, pso_ref,
                 carry_ref, hs_ref, *, layer, mixer, n_prompt_tiles, tiles_per_seq):
    i = pl.program_id(0)
    gd = wp_ref.shape[1]
    nm = nm_ref[layer:layer + 1, :]
    scale = scale_ref[mixer:mixer + 1, :]

    @pl.when(i < n_prompt_tiles)
    def _prompt():
        tm = xp_ref.shape[0]
        halo = POOL_BUF + 1
        x = xp_ref[...]
        h = _rms(x, nm)
        tile_in_seq = i % tiles_per_seq
        prev = jnp.where(tile_in_seq == 0, 0.0, carry_ref[...])
        carry_ref[...] = h[tm - halo:, :]
        pp_ref[0] = h[tm - POOL_BUF:, :]
        pos = tile_in_seq * tm + lax.broadcasted_iota(jnp.int32, (tm, gd), 0)
        p_groups = []
        for g, w in enumerate(POOL_WINDOWS):
            c0, c1 = g * gd, (g + 1) * gd
            hg = h[:, c0:c1]
            acc = jnp.concatenate([prev[:, c0:c1], hg], axis=0)
            span = 1
            while span < w:
                acc = acc[span:] + acc[:-span]
                span *= 2
            total = acc[halo - (w - 1):]
            cnt = jnp.minimum(w, pos + 1).astype(jnp.float32)
            p_groups.append(total / cnt - hg)
        op_ref[...] = x + _pool_project(p_groups, wp_ref, scale)

    @pl.when(i == n_prompt_tiles)
    def _sample():
        n_seq, seq_len, _ = xs_ref.shape
        for t in range(seq_len):
            ht = _rms(xs_ref[:, t, :], nm)
            hs_ref[t] = ht
            pso_ref[:, POOL_BUF - seq_len + t, :] = ht
        for k in range(POOL_BUF - seq_len):
            pso_ref[:, k, :] = ps_ref[:, k + seq_len, :]

        def slab(k, c0, c1):
            return ps_ref[:, k, c0:c1] if k < POOL_BUF else hs_ref[k - POOL_BUF, :, c0:c1]

        for t in range(seq_len):
            last = POOL_BUF + t
            p_groups = []
            for g, w in enumerate(POOL_WINDOWS):
                c0, c1 = g * gd, (g + 1) * gd
                total = slab(last, c0, c1)
                for k in range(last - w + 1, last):
                    total = total + slab(k, c0, c1)
                p_groups.append(total / float(w) - slab(last, c0, c1))
            os_ref[:, t, :] = xs_ref[:, t, :] + _pool_project(p_groups, wp_ref, scale)


def _pool_call(xp, xs, pool_state, norm_mix, wp, scale, *, layer, mixer, seq):
    n, d = xp.shape
    n_seq, seq_len, _ = xs.shape
    tm = TOKEN_TILE
    assert n % tm == 0 and seq % tm == 0 and seq_len <= POOL_BUF and n_seq % SUBLANES == 0
    n_tiles, tiles_per_seq = n // tm, seq // tm
    prompt_map = lambda i: (jnp.minimum(i, n_tiles - 1), 0)
    return pl.pallas_call(
        functools.partial(_pool_kernel, layer=layer, mixer=mixer, n_prompt_tiles=n_tiles,
                          tiles_per_seq=tiles_per_seq),
        grid=(n_tiles + 1,),
        in_specs=[
            pl.BlockSpec((tm, d), prompt_map),
            _const_spec(xs.shape),
            _layer_spec(pool_state.shape, mixer),
            _const_spec(norm_mix.shape),
            _layer_spec(wp.shape, mixer),
            _const_spec(scale.shape),
        ],
        out_specs=[pl.BlockSpec((tm, d), prompt_map),
                   pl.BlockSpec(xs.shape, lambda i: (0, 0, 0)),
                   pl.BlockSpec((1, POOL_BUF, d),
                                lambda i: (jnp.minimum(i, n_tiles - 1) // tiles_per_seq, 0, 0)),
                   pl.BlockSpec((n_seq, POOL_BUF, d), lambda i: (0, 0, 0))],
        out_shape=[jax.ShapeDtypeStruct((n, d), jnp.float32),
                   jax.ShapeDtypeStruct(xs.shape, jnp.float32),
                   jax.ShapeDtypeStruct((n // seq, POOL_BUF, d), jnp.float32),
                   jax.ShapeDtypeStruct((n_seq, POOL_BUF, d), jnp.float32)],
        scratch_shapes=[pltpu.VMEM((POOL_BUF + 1, d), jnp.float32),
                        pltpu.VMEM((seq_len, n_seq, d), jnp.float32)],
        compiler_params=_compiler_params(),
        name="pool_mixer",
    )(xp, xs, pool_state, norm_mix, wp, scale)


def kernel(x_prompt, x_sample, state_pool, state_ffn_conv, norm_mix, norm_ffn, norm_final,
           w_in_a, g_v_a, w_s_a, b_s_a, w_out_a, w_pool_b, scale_b,
           w_gate, w_val, conv_w, conv_b, w_down):
    batch, seq, d = x_prompt.shape
    dec_batch, dec_seq, _ = x_sample.shape
    depth = w_gate.shape[0]
    n_heads = w_s_a.shape[1]
    d_v = w_out_a.shape[1]
    hd = d_v // n_heads
    bf16 = jnp.bfloat16

    w_in_b, w_out_b = w_in_a.astype(bf16), w_out_a.astype(bf16)
    w_pool_bf = w_pool_b.astype(bf16)
    w_gate_b, w_val_b, w_down_b = w_gate.astype(bf16), w_val.astype(bf16), w_down.astype(bf16)

    bias = jnp.repeat(jnp.swapaxes(b_s_a, 1, 2), hd, axis=2)
    ws4 = w_s_a[:, :, :dec_seq, :dec_seq].reshape(-1, n_heads, dec_seq * dec_seq)
    b4 = b_s_a[:, :, :dec_seq]

    xp = x_prompt.reshape(batch * seq, d)
    xs = x_sample
    v_rows, pool_p, pool_s, conv_p, conv_s = [], [], [], [], []
    for i in range(depth):
        j = i // 2
        if i % 2 == 0:
            xp, xs, v = _gmlp_call(xp, xs, norm_mix, w_in_b, g_v_a, w_s_a, bias, ws4[j], b4[j], w_out_b,
                                   layer=i, mixer=j)
            v_rows.append(v)
        else:
            xp, xs, pp, ps = _pool_call(xp, xs, state_pool, norm_mix, w_pool_bf, scale_b,
                                        layer=i, mixer=j, seq=seq)
            pool_p.append(pp)
            pool_s.append(ps)
        xp, xs, cp, cs = _ffn_call(xp, xs, state_ffn_conv, norm_ffn, w_gate_b, w_val_b, conv_w, conv_b,
                                   w_down_b, norm_final if i == depth - 1 else None, layer=i, seq=seq)
        conv_p.append(cp)
        conv_s.append(cs)
    return (xp.reshape(batch, seq, d), xs, jnp.stack(v_rows),
            jnp.stack(pool_p), jnp.stack(pool_s), jnp.stack(conv_p), jnp.stack(conv_s))
```

```python
import functools
import math

import jax
import jax.numpy as jnp
from jax import lax
from jax.experimental import pallas as pl
from jax.experimental.pallas import tpu as pltpu

EPS = 1e-6
CHUNK = 128
POOL_WINDOWS = (2, 4, 8, 16)
POOL_BUF = max(POOL_WINDOWS) - 1
CONV_W = 3
SUBLANES = 8
TOKEN_TILE = 512
FF_CHUNK = 512
VMEM_LIMIT_BYTES = 56 * 1024 * 1024

_GELU_C = math.sqrt(2.0 / math.pi)


def _const_spec(shape):
    zeros = (0,) * len(shape)
    return pl.BlockSpec(shape, lambda i: zeros, pipeline_mode=pl.Buffered(1))


def _layer_spec(stacked_shape, layer):
    tail = (0,) * (len(stacked_shape) - 1)
    return pl.BlockSpec((None,) + tuple(stacked_shape[1:]), lambda i: (layer,) + tail,
                        pipeline_mode=pl.Buffered(1))


def _rms(x, g):
    ms = jnp.mean(x * x, axis=-1, keepdims=True)
    return x * lax.rsqrt(ms + EPS) * g


def _gelu_tanh(x):
    inner = x * (_GELU_C + (_GELU_C * 0.044715) * (x * x))
    return x * (0.5 + 0.5 * jnp.tanh(inner))


def _silu(x):
    return x * (1.0 / (1.0 + jnp.exp(-x)))


def _compiler_params():
    return pltpu.CompilerParams(dimension_semantics=("arbitrary",),
                                vmem_limit_bytes=VMEM_LIMIT_BYTES)


def _load_slabs(ref3d):
    return jnp.concatenate([ref3d[:, t, :] for t in range(ref3d.shape[1])], axis=0)


def _store_slabs(ref3d, val):
    n_seq = ref3d.shape[0]
    for t in range(ref3d.shape[1]):
        ref3d[:, t, :] = val[t * n_seq:(t + 1) * n_seq]


def _gmlp_tile(x, nm_ref, win_ref, gv_ref, wout_ref, m_ref, gate_fn):
    d_v = wout_ref.shape[0]
    h = _rms(x, nm_ref[...]).astype(jnp.bfloat16)
    z = _gelu_tanh(jnp.dot(h, win_ref[...], preferred_element_type=jnp.float32))
    u = z[:, :d_v]
    v = _rms(z[:, d_v:], gv_ref[...])
    gate_fn(u, v)
    y = jnp.dot(m_ref[...], wout_ref[...], preferred_element_type=jnp.float32)
    return x + y, v


def _gmlp_prompt_kernel(x_ref, nm_ref, win_ref, gv_ref, ws_ref, bias_ref, wout_ref, o_ref, m_ref,
                        *, layer, mixer):
    tm = x_ref.shape[0]
    n_heads = ws_ref.shape[0]
    hd = wout_ref.shape[0] // n_heads
    row = lax.broadcasted_iota(jnp.int32, (CHUNK, CHUNK), 0)
    col = lax.broadcasted_iota(jnp.int32, (CHUNK, CHUNK), 1)

    def gate(u, v):
        vb = v.astype(jnp.bfloat16)
        for hh in range(n_heads):
            ws = jnp.where(row >= col, ws_ref[hh], 0.0).astype(jnp.bfloat16)
            c0, c1 = hh * hd, (hh + 1) * hd
            for c in range(tm // CHUNK):
                r0, r1 = c * CHUNK, (c + 1) * CHUNK
                s = jnp.dot(ws, vb[r0:r1, c0:c1], preferred_element_type=jnp.float32)
                s = s + bias_ref[:, c0:c1]
                m_ref[r0:r1, c0:c1] = (u[r0:r1, c0:c1] * s).astype(jnp.bfloat16)

    out, _ = _gmlp_tile(x_ref[...], nm_ref.at[layer:layer + 1, :], win_ref,
                        gv_ref.at[mixer:mixer + 1, :], wout_ref, m_ref, gate)
    o_ref[...] = out


def _gmlp_sample_kernel(xs_ref, nm_ref, win_ref, gv_ref, ws4_ref, b4_ref, wout_ref, os_ref, vs_ref, m_ref,
                        *, layer, mixer):
    n_seq, seq_len, _ = xs_ref.shape
    n_heads = ws4_ref.shape[0]
    hd = wout_ref.shape[0] // n_heads

    def gate(u, v):
        for hh in range(n_heads):
            c0, c1 = hh * hd, (hh + 1) * hd
            vj = [v[j * n_seq:(j + 1) * n_seq, c0:c1] for j in range(seq_len)]
            for t in range(seq_len):
                s = b4_ref[hh, t] + ws4_ref[hh, t * seq_len] * vj[0]
                for j in range(1, t + 1):
                    s = s + ws4_ref[hh, t * seq_len + j] * vj[j]
                r0, r1 = t * n_seq, (t + 1) * n_seq
                m_ref[r0:r1, c0:c1] = (u[r0:r1, c0:c1] * s).astype(jnp.bfloat16)

    out, v = _gmlp_tile(_load_slabs(xs_ref), nm_ref.at[layer:layer + 1, :], win_ref,
                        gv_ref.at[mixer:mixer + 1, :], wout_ref, m_ref, gate)
    _store_slabs(os_ref, out)
    _store_slabs(vs_ref, v)


def _gmlp_prompt_call(xp, norm_mix, w_in, g_v, w_s, bias, w_out, *, layer, mixer):
    n, d = xp.shape
    d_v = w_out.shape[1]
    tm = TOKEN_TILE
    assert n % tm == 0 and tm % CHUNK == 0
    return pl.pallas_call(
        functools.partial(_gmlp_prompt_kernel, layer=layer, mixer=mixer),
        grid=(n // tm,),
        in_specs=[
            pl.BlockSpec((tm, d), lambda i: (i, 0)),
            _const_spec(norm_mix.shape),
            _layer_spec(w_in.shape, mixer),
            _const_spec(g_v.shape),
            _layer_spec(w_s.shape, mixer),
            _layer_spec(bias.shape, mixer),
            _layer_spec(w_out.shape, mixer),
        ],
        out_specs=pl.BlockSpec((tm, d), lambda i: (i, 0)),
        out_shape=jax.ShapeDtypeStruct((n, d), jnp.float32),
        scratch_shapes=[pltpu.VMEM((tm, d_v), jnp.bfloat16)],
        compiler_params=_compiler_params(),
        name="gmlp_mixer",
    )(xp, norm_mix, w_in, g_v, w_s, bias, w_out)


def _gmlp_sample_call(xs, norm_mix, w_in, g_v, ws4, b4, w_out, *, layer, mixer):
    n_seq, seq_len, d = xs.shape
    d_v = w_out.shape[1]
    assert n_seq % SUBLANES == 0
    smem = pl.BlockSpec(memory_space=pltpu.SMEM)
    v_shape = (n_seq, seq_len, d_v)
    return pl.pallas_call(
        functools.partial(_gmlp_sample_kernel, layer=layer, mixer=mixer),
        grid=(1,),
        in_specs=[
            _const_spec(xs.shape),
            _const_spec(norm_mix.shape),
            _layer_spec(w_in.shape, mixer),
            _const_spec(g_v.shape),
            smem, smem,
            _layer_spec(w_out.shape, mixer),
        ],
        out_specs=[pl.BlockSpec(xs.shape, lambda i: (0, 0, 0)), pl.BlockSpec(v_shape, lambda i: (0, 0, 0))],
        out_shape=[jax.ShapeDtypeStruct(xs.shape, jnp.float32), jax.ShapeDtypeStruct(v_shape, jnp.float32)],
        scratch_shapes=[pltpu.VMEM((n_seq * seq_len, d_v), jnp.bfloat16)],
        compiler_params=_compiler_params(),
        name="gmlp_mixer_sample",
    )(xs, norm_mix, w_in, g_v, ws4, b4, w_out)


def _ffn_tile(x, nf_ref, wg_ref, wv_ref, cw_ref, cb_ref, wd_ref, g_ref, history_fn, ng_ref):
    d_ff = wd_ref.shape[0]
    h = _rms(x, nf_ref[...]).astype(jnp.bfloat16)
    bounds = list(range(0, d_ff, FF_CHUNK)) + [d_ff]
    for c0, c1 in zip(bounds[:-1], bounds[1:]):
        a = jnp.dot(h, wg_ref[:, c0:c1], preferred_element_type=jnp.float32)
        val = jnp.dot(h, wv_ref[:, c0:c1], preferred_element_type=jnp.float32)
        prev1, prev2 = history_fn(a, c0, c1)
        conv = (cb_ref[:, c0:c1] + cw_ref[0:1, c0:c1] * prev2
                + cw_ref[1:2, c0:c1] * prev1 + cw_ref[2:3, c0:c1] * a)
        g_ref[:, c0:c1] = (_silu(conv) * val).astype(jnp.bfloat16)
    y = jnp.dot(g_ref[...], wd_ref[...], preferred_element_type=jnp.float32)
    out = x + y
    return out if ng_ref is None else _rms(out, ng_ref[...])


def _ffn_prompt_kernel(*refs, layer, tiles_per_seq, final):
    it = iter(refs)
    x_ref, nf_ref, wg_ref, wv_ref, cw_ref, cbias_ref, wd_ref = (next(it) for _ in range(7))
    ng_ref = next(it) if final else None
    o_ref, cp_ref, g_ref, carry_ref = (next(it) for _ in range(4))
    tm = x_ref.shape[0]
    keep = CONV_W - 1
    first = pl.program_id(0) % tiles_per_seq == 0

    def history(a, c0, c1):
        head = jnp.where(first, 0.0, carry_ref[:, c0:c1])
        carry_ref[:, c0:c1] = a[tm - SUBLANES:, :]
        cp_ref[0, :, c0:c1] = a[tm - keep:, :]
        ext = jnp.concatenate([head, a], axis=0)
        return ext[SUBLANES - 1:SUBLANES - 1 + tm], ext[SUBLANES - 2:SUBLANES - 2 + tm]

    o_ref[...] = _ffn_tile(x_ref[...], nf_ref.at[layer:layer + 1, :], wg_ref, wv_ref, cw_ref,
                           cbias_ref.at[layer:layer + 1, :], wd_ref, g_ref, history, ng_ref)


def _ffn_sample_kernel(*refs, layer, final):
    it = iter(refs)
    xs_ref, cs_ref, nf_ref, wg_ref, wv_ref, cw_ref, cbias_ref, wd_ref = (next(it) for _ in range(8))
    ng_ref = next(it) if final else None
    os_ref, cso_ref, g_ref = (next(it) for _ in range(3))
    n_seq, seq_len, _ = xs_ref.shape
    rows = n_seq * seq_len
    keep = CONV_W - 1

    def history(a, c0, c1):
        old = [cs_ref[:, k, c0:c1] for k in range(keep)]
        for k in range(keep):
            t = seq_len - keep + k
            cso_ref[:, k, c0:c1] = a[t * n_seq:(t + 1) * n_seq]
        prev1 = jnp.concatenate([old[1], a[:rows - n_seq]], axis=0)
        prev2 = jnp.concatenate([old[0], old[1], a[:rows - 2 * n_seq]], axis=0)
        return prev1, prev2

    out = _ffn_tile(_load_slabs(xs_ref), nf_ref.at[layer:layer + 1, :], wg_ref, wv_ref, cw_ref,
                    cbias_ref.at[layer:layer + 1, :], wd_ref, g_ref, history, ng_ref)
    _store_slabs(os_ref, out)


def _ffn_weight_specs(norm_ffn, wg, wv, cw, cb, wd, norm_final, layer):
    args = [norm_ffn, wg, wv, cw, cb, wd]
    specs = [_const_spec(norm_ffn.shape), _layer_spec(wg.shape, layer), _layer_spec(wv.shape, layer),
             _layer_spec(cw.shape, layer), _const_spec(cb.shape), _layer_spec(wd.shape, layer)]
    if norm_final is not None:
        args.append(norm_final.reshape(1, -1))
        specs.append(_const_spec((1, norm_final.shape[0])))
    return args, specs


def _ffn_prompt_call(xp, norm_ffn, wg, wv, cw, cb, wd, norm_final, *, layer, seq):
    n, d = xp.shape
    d_ff = wd.shape[1]
    tm = TOKEN_TILE
    assert n % tm == 0 and seq % tm == 0
    tiles_per_seq = seq // tm
    keep = CONV_W - 1
    w_args, w_specs = _ffn_weight_specs(norm_ffn, wg, wv, cw, cb, wd, norm_final, layer)
    return pl.pallas_call(
        functools.partial(_ffn_prompt_kernel, layer=layer, tiles_per_seq=tiles_per_seq,
                          final=norm_final is not None),
        grid=(n // tm,),
        in_specs=[pl.BlockSpec((tm, d), lambda i: (i, 0))] + w_specs,
        out_specs=[pl.BlockSpec((tm, d), lambda i: (i, 0)),
                   pl.BlockSpec((1, keep, d_ff), lambda i: (i // tiles_per_seq, 0, 0))],
        out_shape=[jax.ShapeDtypeStruct((n, d), jnp.float32),
                   jax.ShapeDtypeStruct((n // seq, keep, d_ff), jnp.float32)],
        scratch_shapes=[pltpu.VMEM((tm, d_ff), jnp.bfloat16),
                        pltpu.VMEM((SUBLANES, d_ff), jnp.float32)],
        compiler_params=_compiler_params(),
        name="conv_ffn",
    )(xp, *w_args)


def _ffn_sample_call(xs, conv_state, norm_ffn, wg, wv, cw, cb, wd, norm_final, *, layer):
    n_seq, seq_len, d = xs.shape
    d_ff = wd.shape[1]
    keep = CONV_W - 1
    assert seq_len >= keep and n_seq % SUBLANES == 0
    w_args, w_specs = _ffn_weight_specs(norm_ffn, wg, wv, cw, cb, wd, norm_final, layer)
    return pl.pallas_call(
        functools.partial(_ffn_sample_kernel, layer=layer, final=norm_final is not None),
        grid=(1,),
        in_specs=[_const_spec(xs.shape), _layer_spec(conv_state.shape, layer)] + w_specs,
        out_specs=[pl.BlockSpec(xs.shape, lambda i: (0, 0, 0)),
                   pl.BlockSpec((n_seq, keep, d_ff), lambda i: (0, 0, 0))],
        out_shape=[jax.ShapeDtypeStruct(xs.shape, jnp.float32),
                   jax.ShapeDtypeStruct((n_seq, keep, d_ff), jnp.float32)],
        scratch_shapes=[pltpu.VMEM((n_seq * seq_len, d_ff), jnp.bfloat16)],
        compiler_params=_compiler_params(),
        name="conv_ffn_sample",
    )(xs, conv_state, *w_args)


def _pool_project(p_groups, wp_ref, scale_ref):
    gd = wp_ref.shape[1]
    ys = [jnp.dot(p.astype(jnp.bfloat16), wp_ref[g], preferred_element_type=jnp.float32)
          * scale_ref[:, g * gd:(g + 1) * gd] for g, p in enumerate(p_groups)]
    return jnp.concatenate(ys, axis=-1)


def _pool_prompt_kernel(x_ref, nm_ref, wp_ref, scale_ref, o_ref, pp_ref, carry_ref,
                        *, layer, mixer, tiles_per_seq):
    gd = wp_ref.shape[1]
    tm = x_ref.shape[0]
    halo = POOL_BUF + 1
    x = x_ref[...]
    h = _rms(x, nm_ref[layer:layer + 1, :])
    tile_in_seq = pl.program_id(0) % tiles_per_seq
    prev = jnp.where(tile_in_seq == 0, 0.0, carry_ref[...])
    carry_ref[...] = h[tm - halo:, :]
    pp_ref[0] = h[tm - POOL_BUF:, :]
    pos = tile_in_seq * tm + lax.broadcasted_iota(jnp.int32, (tm, gd), 0)
    p_groups = []
    for g, w in enumerate(POOL_WINDOWS):
        c0, c1 = g * gd, (g + 1) * gd
        hg = h[:, c0:c1]
        acc = jnp.concatenate([prev[:, c0:c1], hg], axis=0)
        span = 1
        while span < w:
            acc = acc[span:] + acc[:-span]
            span *= 2
        total = acc[halo - (w - 1):]
        cnt = jnp.minimum(w, pos + 1).astype(jnp.float32)
        p_groups.append(total / cnt - hg)
    o_ref[...] = x + _pool_project(p_groups, wp_ref, scale_ref.at[mixer:mixer + 1, :])


def _pool_sample_kernel(xs_ref, ps_ref, nm_ref, wp_ref, scale_ref, os_ref, pso_ref, rows_ref, *, layer, mixer):
    gd = wp_ref.shape[1]
    n_seq, seq_len, _ = xs_ref.shape
    for k in range(POOL_BUF):
        rows_ref[k] = ps_ref[:, k, :]
    for t in range(seq_len):
        rows_ref[POOL_BUF + t] = _rms(xs_ref[:, t, :], nm_ref[layer:layer + 1, :])
    for k in range(POOL_BUF):
        pso_ref[:, k, :] = rows_ref[k + seq_len]
    for t in range(seq_len):
        last = POOL_BUF + t
        p_groups = []
        for g, w in enumerate(POOL_WINDOWS):
            c0, c1 = g * gd, (g + 1) * gd
            total = rows_ref[last, :, c0:c1]
            for k in range(last - w + 1, last):
                total = total + rows_ref[k, :, c0:c1]
            p_groups.append(total / float(w) - rows_ref[last, :, c0:c1])
        os_ref[:, t, :] = xs_ref[:, t, :] + _pool_project(p_groups, wp_ref, scale_ref.at[mixer:mixer + 1, :])


def _pool_prompt_call(xp, norm_mix, wp, scale, *, layer, mixer, seq):
    n, d = xp.shape
    tm = TOKEN_TILE
    assert n % tm == 0 and seq % tm == 0
    tiles_per_seq = seq // tm
    return pl.pallas_call(
        functools.partial(_pool_prompt_kernel, layer=layer, mixer=mixer, tiles_per_seq=tiles_per_seq),
        grid=(n // tm,),
        in_specs=[
            pl.BlockSpec((tm, d), lambda i: (i, 0)),
            _const_spec(norm_mix.shape),
            _layer_spec(wp.shape, mixer),
            _const_spec(scale.shape),
        ],
        out_specs=[pl.BlockSpec((tm, d), lambda i: (i, 0)),
                   pl.BlockSpec((1, POOL_BUF, d), lambda i: (i // tiles_per_seq, 0, 0))],
        out_shape=[jax.ShapeDtypeStruct((n, d), jnp.float32),
                   jax.ShapeDtypeStruct((n // seq, POOL_BUF, d), jnp.float32)],
        scratch_shapes=[pltpu.VMEM((POOL_BUF + 1, d), jnp.float32)],
        compiler_params=_compiler_params(),
        name="pool_mixer",
    )(xp, norm_mix, wp, scale)


def _pool_sample_call(xs, pool_state, norm_mix, wp, scale, *, layer, mixer):
    n_seq, seq_len, d = xs.shape
    assert seq_len <= POOL_BUF and n_seq % SUBLANES == 0
    return pl.pallas_call(
        functools.partial(_pool_sample_kernel, layer=layer, mixer=mixer),
        grid=(1,),
        in_specs=[
            _const_spec(xs.shape),
            _layer_spec(pool_state.shape, mixer),
            _const_spec(norm_mix.shape),
            _layer_spec(wp.shape, mixer),
            _const_spec(scale.shape),
        ],
        out_specs=[pl.BlockSpec(xs.shape, lambda i: (0, 0, 0)),
                   pl.BlockSpec((n_seq, POOL_BUF, d), lambda i: (0, 0, 0))],
        out_shape=[jax.ShapeDtypeStruct(xs.shape, jnp.float32),
                   jax.ShapeDtypeStruct((n_seq, POOL_BUF, d), jnp.float32)],
        scratch_shapes=[pltpu.VMEM((POOL_BUF + seq_len, n_seq, d), jnp.float32)],
        compiler_params=_compiler_params(),
        name="pool_mixer_sample",
    )(xs, pool_state, norm_mix, wp, scale)


def kernel(x_prompt, x_sample, state_pool, state_ffn_conv, norm_mix, norm_ffn, norm_final,
           w_in_a, g_v_a, w_s_a, b_s_a, w_out_a, w_pool_b, scale_b,
           w_gate, w_val, conv_w, conv_b, w_down):
    batch, seq, d = x_prompt.shape
    dec_batch, dec_seq, _ = x_sample.shape
    depth = w_gate.shape[0]
    n_heads = w_s_a.shape[1]
    d_v = w_out_a.shape[1]
    hd = d_v // n_heads
    bf16 = jnp.bfloat16

    w_in_b, w_out_b = w_in_a.astype(bf16), w_out_a.astype(bf16)
    w_pool_bf = w_pool_b.astype(bf16)
    w_gate_b, w_val_b, w_down_b = w_gate.astype(bf16), w_val.astype(bf16), w_down.astype(bf16)

    bias = jnp.repeat(jnp.swapaxes(b_s_a, 1, 2), hd, axis=2)
    ws4 = w_s_a[:, :, :dec_seq, :dec_seq].reshape(-1, n_heads, dec_seq * dec_seq)
    b4 = b_s_a[:, :, :dec_seq]

    xp = x_prompt.reshape(batch * seq, d)
    xs = x_sample
    v_rows, pool_p, pool_s, conv_p, conv_s = [], [], [], [], []
    for i in range(depth):
        j = i // 2
        nfin = norm_final if i == depth - 1 else None
        if i % 2 == 0:
            xp = _gmlp_prompt_call(xp, norm_mix, w_in_b, g_v_a, w_s_a, bias, w_out_b, layer=i, mixer=j)
            xs, v = _gmlp_sample_call(xs, norm_mix, w_in_b, g_v_a, ws4[j], b4[j], w_out_b, layer=i, mixer=j)
            v_rows.append(v)
        else:
            xp, pp = _pool_prompt_call(xp, norm_mix, w_pool_bf, scale_b, layer=i, mixer=j, seq=seq)
            xs, ps = _pool_sample_call(xs, state_pool, norm_mix, w_pool_bf, scale_b, layer=i, mixer=j)
            pool_p.append(pp)
            pool_s.append(ps)
        xp, cp = _ffn_prompt_call(xp, norm_ffn, w_gate_b, w_val_b, conv_w, conv_b, w_down_b, nfin,
                                  layer=i, seq=seq)
        xs, cs = _ffn_sample_call(xs, state_ffn_conv, norm_ffn, w_gate_b, w_val_b, conv_w, conv_b,
                                  w_down_b, nfin, layer=i)
        conv_p.append(cp)
        conv_s.append(cs)
    return (xp.reshape(batch, seq, d), xs, jnp.stack(v_rows),
            jnp.stack(pool_p), jnp.stack(pool_s), jnp.stack(conv_p), jnp.stack(conv_s))
```

```python
import functools
import math

import jax
import jax.numpy as jnp
from jax import lax
from jax.experimental import pallas as pl
from jax.experimental.pallas import tpu as pltpu

EPS = 1e-6
CHUNK = 128
POOL_WINDOWS = (2, 4, 8, 16)
POOL_BUF = max(POOL_WINDOWS) - 1
CONV_W = 3
SUBLANES = 8
TOKEN_TILE = 512
FF_CHUNK = 512
GMLP_CHUNK = 1024
VMEM_LIMIT_BYTES = 56 * 1024 * 1024

_GELU_C = math.sqrt(2.0 / math.pi)


def _const_spec(shape):
    zeros = (0,) * len(shape)
    return pl.BlockSpec(shape, lambda i: zeros, pipeline_mode=pl.Buffered(1))


def _layer_spec(stacked_shape, layer):
    tail = (0,) * (len(stacked_shape) - 1)
    return pl.BlockSpec((None,) + tuple(stacked_shape[1:]), lambda i: (layer,) + tail,
                        pipeline_mode=pl.Buffered(1))


def _rms(x, g):
    ms = jnp.mean(x * x, axis=-1, keepdims=True)
    return x * lax.rsqrt(ms + EPS) * g


def _gelu_tanh(x):
    inner = x * (_GELU_C + (_GELU_C * 0.044715) * (x * x))
    return x * (0.5 + 0.5 * jnp.tanh(inner))


def _silu(x):
    return x * (1.0 / (1.0 + jnp.exp(-x)))


def _compiler_params():
    return pltpu.CompilerParams(dimension_semantics=("arbitrary",),
                                vmem_limit_bytes=VMEM_LIMIT_BYTES)


def _load_slabs(ref3d):
    return jnp.concatenate([ref3d[:, t, :] for t in range(ref3d.shape[1])], axis=0)


def _store_slabs(ref3d, val):
    n_seq = ref3d.shape[0]
    for t in range(ref3d.shape[1]):
        ref3d[:, t, :] = val[t * n_seq:(t + 1) * n_seq]


def _gmlp_tile(x, nm_ref, win_ref, gv_ref, wout_ref, v_ref, m_ref, gate_fn):
    d_v = wout_ref.shape[0]
    h = _rms(x, nm_ref[...]).astype(jnp.bfloat16)
    chunks = [(c0, min(c0 + GMLP_CHUNK, d_v)) for c0 in range(0, d_v, GMLP_CHUNK)]
    ssq = None
    for c0, c1 in chunks:
        zc = _gelu_tanh(jnp.dot(h, win_ref[:, d_v + c0:d_v + c1], preferred_element_type=jnp.float32))
        v_ref[:, c0:c1] = zc
        part = jnp.sum(zc * zc, axis=-1, keepdims=True)
        ssq = part if ssq is None else ssq + part
    rinv = lax.rsqrt(ssq * (1.0 / d_v) + EPS)
    for c0, c1 in chunks:
        uc = _gelu_tanh(jnp.dot(h, win_ref[:, c0:c1], preferred_element_type=jnp.float32))
        vc = v_ref[:, c0:c1] * rinv * gv_ref[:, c0:c1]
        gate_fn(uc, vc, c0, c1)
    y = jnp.dot(m_ref[...], wout_ref[...], preferred_element_type=jnp.float32)
    return x + y


def _gmlp_prompt_kernel(x_ref, nm_ref, win_ref, gv_ref, ws_ref, bias_ref, wout_ref, o_ref, v_ref, m_ref,
                        *, layer, mixer):
    tm = x_ref.shape[0]
    hd = wout_ref.shape[0] // ws_ref.shape[0]
    row = lax.broadcasted_iota(jnp.int32, (CHUNK, CHUNK), 0)
    col = lax.broadcasted_iota(jnp.int32, (CHUNK, CHUNK), 1)

    def gate(uc, vc, c0, c1):
        vb = vc.astype(jnp.bfloat16)
        for hh in range(c0 // hd, c1 // hd):
            ws = jnp.where(row >= col, ws_ref[hh], 0.0).astype(jnp.bfloat16)
            l0, l1 = hh * hd - c0, (hh + 1) * hd - c0
            for c in range(tm // CHUNK):
                r0, r1 = c * CHUNK, (c + 1) * CHUNK
                s = jnp.dot(ws, vb[r0:r1, l0:l1], preferred_element_type=jnp.float32)
                s = s + bias_ref[:, c0 + l0:c0 + l1]
                m_ref[r0:r1, c0 + l0:c0 + l1] = (uc[r0:r1, l0:l1] * s).astype(jnp.bfloat16)

    o_ref[...] = _gmlp_tile(x_ref[...], nm_ref.at[layer:layer + 1, :], win_ref,
                            gv_ref.at[mixer:mixer + 1, :], wout_ref, v_ref, m_ref, gate)


def _gmlp_sample_kernel(xs_ref, nm_ref, win_ref, gv_ref, ws4_ref, b4_ref, wout_ref, os_ref, vs_ref,
                        v_ref, m_ref, *, layer, mixer):
    n_seq, seq_len, _ = xs_ref.shape
    hd = wout_ref.shape[0] // ws4_ref.shape[0]

    def gate(uc, vc, c0, c1):
        for t in range(seq_len):
            vs_ref[:, t, c0:c1] = vc[t * n_seq:(t + 1) * n_seq]
        for hh in range(c0 // hd, c1 // hd):
            l0, l1 = hh * hd - c0, (hh + 1) * hd - c0
            vj = [vc[j * n_seq:(j + 1) * n_seq, l0:l1] for j in range(seq_len)]
            for t in range(seq_len):
                s = b4_ref[hh, t] + ws4_ref[hh, t * seq_len] * vj[0]
                for j in range(1, t + 1):
                    s = s + ws4_ref[hh, t * seq_len + j] * vj[j]
                r0, r1 = t * n_seq, (t + 1) * n_seq
                m_ref[r0:r1, c0 + l0:c0 + l1] = (uc[r0:r1, l0:l1] * s).astype(jnp.bfloat16)

    out = _gmlp_tile(_load_slabs(xs_ref), nm_ref.at[layer:layer + 1, :], win_ref,
                     gv_ref.at[mixer:mixer + 1, :], wout_ref, v_ref, m_ref, gate)
    _store_slabs(os_ref, out)


def _gmlp_prompt_call(xp, norm_mix, w_in, g_v, w_s, bias, w_out, *, layer, mixer):
    n, d = xp.shape
    d_v = w_out.shape[1]
    tm = TOKEN_TILE
    assert n % tm == 0 and tm % CHUNK == 0
    return pl.pallas_call(
        functools.partial(_gmlp_prompt_kernel, layer=layer, mixer=mixer),
        grid=(n // tm,),
        in_specs=[
            pl.BlockSpec((tm, d), lambda i: (i, 0)),
            _const_spec(norm_mix.shape),
            _layer_spec(w_in.shape, mixer),
            _const_spec(g_v.shape),
            _layer_spec(w_s.shape, mixer),
            _layer_spec(bias.shape, mixer),
            _layer_spec(w_out.shape, mixer),
        ],
        out_specs=pl.BlockSpec((tm, d), lambda i: (i, 0)),
        out_shape=jax.ShapeDtypeStruct((n, d), jnp.float32),
        scratch_shapes=[pltpu.VMEM((tm, d_v), jnp.float32), pltpu.VMEM((tm, d_v), jnp.bfloat16)],
        compiler_params=_compiler_params(),
        name="gmlp_mixer",
    )(xp, norm_mix, w_in, g_v, w_s, bias, w_out)


def _gmlp_sample_call(xs, norm_mix, w_in, g_v, ws4, b4, w_out, *, layer, mixer):
    n_seq, seq_len, d = xs.shape
    d_v = w_out.shape[1]
    assert n_seq % SUBLANES == 0
    smem = pl.BlockSpec(memory_space=pltpu.SMEM)
    v_shape = (n_seq, seq_len, d_v)
    return pl.pallas_call(
        functools.partial(_gmlp_sample_kernel, layer=layer, mixer=mixer),
        grid=(1,),
        in_specs=[
            _const_spec(xs.shape),
            _const_spec(norm_mix.shape),
            _layer_spec(w_in.shape, mixer),
            _const_spec(g_v.shape),
            smem, smem,
            _layer_spec(w_out.shape, mixer),
        ],
        out_specs=[pl.BlockSpec(xs.shape, lambda i: (0, 0, 0)), pl.BlockSpec(v_shape, lambda i: (0, 0, 0))],
        out_shape=[jax.ShapeDtypeStruct(xs.shape, jnp.float32), jax.ShapeDtypeStruct(v_shape, jnp.float32)],
        scratch_shapes=[pltpu.VMEM((n_seq * seq_len, d_v), jnp.float32),
                        pltpu.VMEM((n_seq * seq_len, d_v), jnp.bfloat16)],
        compiler_params=_compiler_params(),
        name="gmlp_mixer_sample",
    )(xs, norm_mix, w_in, g_v, ws4, b4, w_out)


def _ffn_tile(x, nf_ref, wg_ref, wv_ref, cw_ref, cb_ref, wd_ref, g_ref, history_fn, ng_ref):
    d_ff = wd_ref.shape[0]
    h = _rms(x, nf_ref[...]).astype(jnp.bfloat16)
    bounds = list(range(0, d_ff, FF_CHUNK)) + [d_ff]
    for c0, c1 in zip(bounds[:-1], bounds[1:]):
        a = jnp.dot(h, wg_ref[:, c0:c1], preferred_element_type=jnp.float32)
        val = jnp.dot(h, wv_ref[:, c0:c1], preferred_element_type=jnp.float32)
        prev1, prev2 = history_fn(a, c0, c1)
        conv = (cb_ref[:, c0:c1] + cw_ref[0:1, c0:c1] * prev2
                + cw_ref[1:2, c0:c1] * prev1 + cw_ref[2:3, c0:c1] * a)
        g_ref[:, c0:c1] = (_silu(conv) * val).astype(jnp.bfloat16)
    y = jnp.dot(g_ref[...], wd_ref[...], preferred_element_type=jnp.float32)
    out = x + y
    return out if ng_ref is None else _rms(out, ng_ref[...])


def _ffn_prompt_kernel(*refs, layer, tiles_per_seq, final):
    it = iter(refs)
    x_ref, nf_ref, wg_ref, wv_ref, cw_ref, cbias_ref, wd_ref = (next(it) for _ in range(7))
    ng_ref = next(it) if final else None
    o_ref, cp_ref, g_ref, carry_ref = (next(it) for _ in range(4))
    tm = x_ref.shape[0]
    keep = CONV_W - 1

    @pl.when(pl.program_id(0) % tiles_per_seq == 0)
    def _new_sequence():
        carry_ref[...] = jnp.zeros_like(carry_ref)

    def history(a, c0, c1):
        head = carry_ref[:, c0:c1]
        carry_ref[:, c0:c1] = a[tm - SUBLANES:, :]
        cp_ref[0, :, c0:c1] = a[tm - keep:, :]
        ext = jnp.concatenate([head, a], axis=0)
        return ext[SUBLANES - 1:SUBLANES - 1 + tm], ext[SUBLANES - 2:SUBLANES - 2 + tm]

    o_ref[...] = _ffn_tile(x_ref[...], nf_ref.at[layer:layer + 1, :], wg_ref, wv_ref, cw_ref,
                           cbias_ref.at[layer:layer + 1, :], wd_ref, g_ref, history, ng_ref)


def _ffn_sample_kernel(*refs, layer, final):
    it = iter(refs)
    xs_ref, cs_ref, nf_ref, wg_ref, wv_ref, cw_ref, cbias_ref, wd_ref = (next(it) for _ in range(8))
    ng_ref = next(it) if final else None
    os_ref, cso_ref, g_ref = (next(it) for _ in range(3))
    n_seq, seq_len, _ = xs_ref.shape
    rows = n_seq * seq_len
    keep = CONV_W - 1

    def history(a, c0, c1):
        old = [cs_ref[:, k, c0:c1] for k in range(keep)]
        for k in range(keep):
            t = seq_len - keep + k
            cso_ref[:, k, c0:c1] = a[t * n_seq:(t + 1) * n_seq]
        prev1 = jnp.concatenate([old[1], a[:rows - n_seq]], axis=0)
        prev2 = jnp.concatenate([old[0], old[1], a[:rows - 2 * n_seq]], axis=0)
        return prev1, prev2

    out = _ffn_tile(_load_slabs(xs_ref), nf_ref.at[layer:layer + 1, :], wg_ref, wv_ref, cw_ref,
                    cbias_ref.at[layer:layer + 1, :], wd_ref, g_ref, history, ng_ref)
    _store_slabs(os_ref, out)


def _ffn_weight_specs(norm_ffn, wg, wv, cw, cb, wd, norm_final, layer):
    args = [norm_ffn, wg, wv, cw, cb, wd]
    specs = [_const_spec(norm_ffn.shape), _layer_spec(wg.shape, layer), _layer_spec(wv.shape, layer),
             _layer_spec(cw.shape, layer), _const_spec(cb.shape), _layer_spec(wd.shape, layer)]
    if norm_final is not None:
        args.append(norm_final.reshape(1, -1))
        specs.append(_const_spec((1, norm_final.shape[0])))
    return args, specs


def _ffn_prompt_call(xp, norm_ffn, wg, wv, cw, cb, wd, norm_final, *, layer, seq):
    n, d = xp.shape
    d_ff = wd.shape[1]
    tm = TOKEN_TILE
    assert n % tm == 0 and seq % tm == 0
    tiles_per_seq = seq // tm
    keep = CONV_W - 1
    w_args, w_specs = _ffn_weight_specs(norm_ffn, wg, wv, cw, cb, wd, norm_final, layer)
    return pl.pallas_call(
        functools.partial(_ffn_prompt_kernel, layer=layer, tiles_per_seq=tiles_per_seq,
                          final=norm_final is not None),
        grid=(n // tm,),
        in_specs=[pl.BlockSpec((tm, d), lambda i: (i, 0))] + w_specs,
        out_specs=[pl.BlockSpec((tm, d), lambda i: (i, 0)),
                   pl.BlockSpec((1, keep, d_ff), lambda i: (i // tiles_per_seq, 0, 0))],
        out_shape=[jax.ShapeDtypeStruct((n, d), jnp.float32),
                   jax.ShapeDtypeStruct((n // seq, keep, d_ff), jnp.float32)],
        scratch_shapes=[pltpu.VMEM((tm, d_ff), jnp.bfloat16),
                        pltpu.VMEM((SUBLANES, d_ff), jnp.float32)],
        compiler_params=_compiler_params(),
        name="conv_ffn",
    )(xp, *w_args)


def _ffn_sample_call(xs, conv_state, norm_ffn, wg, wv, cw, cb, wd, norm_final, *, layer):
    n_seq, seq_len, d = xs.shape
    d_ff = wd.shape[1]
    keep = CONV_W - 1
    assert seq_len >= keep and n_seq % SUBLANES == 0
    w_args, w_specs = _ffn_weight_specs(norm_ffn, wg, wv, cw, cb, wd, norm_final, layer)
    return pl.pallas_call(
        functools.partial(_ffn_sample_kernel, layer=layer, final=norm_final is not None),
        grid=(1,),
        in_specs=[_const_spec(xs.shape), _layer_spec(conv_state.shape, layer)] + w_specs,
        out_specs=[pl.BlockSpec(xs.shape, lambda i: (0, 0, 0)),
                   pl.BlockSpec((n_seq, keep, d_ff), lambda i: (0, 0, 0))],
        out_shape=[jax.ShapeDtypeStruct(xs.shape, jnp.float32),
                   jax.ShapeDtypeStruct((n_seq, keep, d_ff), jnp.float32)],
        scratch_shapes=[pltpu.VMEM((n_seq * seq_len, d_ff), jnp.bfloat16)],
        compiler_params=_compiler_params(),
        name="conv_ffn_sample",
    )(xs, conv_state, *w_args)


def _pool_project(p_groups, wp_ref, scale_ref):
    gd = wp_ref.shape[1]
    ys = [jnp.dot(p.astype(jnp.bfloat16), wp_ref[g], preferred_element_type=jnp.float32)
          * scale_ref[:, g * gd:(g + 1) * gd] for g, p in enumerate(p_groups)]
    return jnp.concatenate(ys, axis=-1)


def _pool_prompt_kernel(x_ref, nm_ref, wp_ref, scale_ref, o_ref, pp_ref, carry_ref,
                        *, layer, mixer, tiles_per_seq):
    gd = wp_ref.shape[1]
    tm = x_ref.shape[0]
    halo = POOL_BUF + 1
    x = x_ref[...]
    h = _rms(x, nm_ref[layer:layer + 1, :])
    tile_in_seq = pl.program_id(0) % tiles_per_seq

    @pl.when(tile_in_seq == 0)
    def _new_sequence():
        carry_ref[...] = jnp.zeros_like(carry_ref)

    prev = carry_ref[...]
    carry_ref[...] = h[tm - halo:, :]
    pp_ref[0] = h[tm - POOL_BUF:, :]
    pos = tile_in_seq * tm + lax.broadcasted_iota(jnp.int32, (tm, gd), 0)
    p_groups = []
    for g, w in enumerate(POOL_WINDOWS):
        c0, c1 = g * gd, (g + 1) * gd
        hg = h[:, c0:c1]
        acc = jnp.concatenate([prev[:, c0:c1], hg], axis=0)
        span = 1
        while span < w:
            acc = acc[span:] + acc[:-span]
            span *= 2
        total = acc[halo - (w - 1):]
        cnt = jnp.minimum(w, pos + 1).astype(jnp.float32)
        p_groups.append(total / cnt - hg)
    o_ref[...] = x + _pool_project(p_groups, wp_ref, scale_ref.at[mixer:mixer + 1, :])


def _pool_sample_kernel(xs_ref, ps_ref, nm_ref, wp_ref, scale_ref, os_ref, pso_ref, hs_ref, *, layer, mixer):
    gd = wp_ref.shape[1]
    n_seq, seq_len, _ = xs_ref.shape
    for t in range(seq_len):
        ht = _rms(xs_ref[:, t, :], nm_ref[layer:layer + 1, :])
        hs_ref[t] = ht
        pso_ref[POOL_BUF - seq_len + t] = ht
    for k in range(POOL_BUF - seq_len):
        pso_ref[k] = ps_ref[k + seq_len]

    def slab(k, c0, c1):
        return ps_ref[k, :, c0:c1] if k < POOL_BUF else hs_ref[k - POOL_BUF, :, c0:c1]

    for t in range(seq_len):
        last = POOL_BUF + t
        p_groups = []
        for g, w in enumerate(POOL_WINDOWS):
            c0, c1 = g * gd, (g + 1) * gd
            total = slab(last, c0, c1)
            for k in range(last - w + 1, last):
                total = total + slab(k, c0, c1)
            p_groups.append(total / float(w) - slab(last, c0, c1))
        os_ref[:, t, :] = xs_ref[:, t, :] + _pool_project(p_groups, wp_ref, scale_ref.at[mixer:mixer + 1, :])


def _pool_prompt_call(xp, norm_mix, wp, scale, *, layer, mixer, seq):
    n, d = xp.shape
    tm = TOKEN_TILE
    assert n % tm == 0 and seq % tm == 0
    tiles_per_seq = seq // tm
    return pl.pallas_call(
        functools.partial(_pool_prompt_kernel, layer=layer, mixer=mixer, tiles_per_seq=tiles_per_seq),
        grid=(n // tm,),
        in_specs=[
            pl.BlockSpec((tm, d), lambda i: (i, 0)),
            _const_spec(norm_mix.shape),
            _layer_spec(wp.shape, mixer),
            _const_spec(scale.shape),
        ],
        out_specs=[pl.BlockSpec((tm, d), lambda i: (i, 0)),
                   pl.BlockSpec((1, POOL_BUF, d), lambda i: (i // tiles_per_seq, 0, 0))],
        out_shape=[jax.ShapeDtypeStruct((n, d), jnp.float32),
                   jax.ShapeDtypeStruct((n // seq, POOL_BUF, d), jnp.float32)],
        scratch_shapes=[pltpu.VMEM((POOL_BUF + 1, d), jnp.float32)],
        compiler_params=_compiler_params(),
        name="pool_mixer",
    )(xp, norm_mix, wp, scale)


def _pool_sample_call(xs, pool_state_tm, norm_mix, wp, scale, *, layer, mixer):
    n_seq, seq_len, d = xs.shape
    assert seq_len <= POOL_BUF and n_seq % SUBLANES == 0
    return pl.pallas_call(
        functools.partial(_pool_sample_kernel, layer=layer, mixer=mixer),
        grid=(1,),
        in_specs=[
            _const_spec(xs.shape),
            _layer_spec(pool_state_tm.shape, mixer),
            _const_spec(norm_mix.shape),
            _layer_spec(wp.shape, mixer),
            _const_spec(scale.shape),
        ],
        out_specs=[pl.BlockSpec(xs.shape, lambda i: (0, 0, 0)),
                   pl.BlockSpec((POOL_BUF, n_seq, d), lambda i: (0, 0, 0))],
        out_shape=[jax.ShapeDtypeStruct(xs.shape, jnp.float32),
                   jax.ShapeDtypeStruct((POOL_BUF, n_seq, d), jnp.float32)],
        scratch_shapes=[pltpu.VMEM((seq_len, n_seq, d), jnp.float32)],
        compiler_params=_compiler_params(),
        name="pool_mixer_sample",
    )(xs, pool_state_tm, norm_mix, wp, scale)


def kernel(x_prompt, x_sample, state_pool, state_ffn_conv, norm_mix, norm_ffn, norm_final,
           w_in_a, g_v_a, w_s_a, b_s_a, w_out_a, w_pool_b, scale_b,
           w_gate, w_val, conv_w, conv_b, w_down):
    batch, seq, d = x_prompt.shape
    dec_batch, dec_seq, _ = x_sample.shape
    depth = w_gate.shape[0]
    n_heads = w_s_a.shape[1]
    d_v = w_out_a.shape[1]
    hd = d_v // n_heads
    bf16 = jnp.bfloat16

    w_in_b, w_out_b = w_in_a.astype(bf16), w_out_a.astype(bf16)
    w_pool_bf = w_pool_b.astype(bf16)
    w_gate_b, w_val_b, w_down_b = w_gate.astype(bf16), w_val.astype(bf16), w_down.astype(bf16)

    bias = jnp.repeat(jnp.swapaxes(b_s_a, 1, 2), hd, axis=2)
    ws4 = w_s_a[:, :, :dec_seq, :dec_seq].reshape(-1, n_heads, dec_seq * dec_seq)
    b4 = b_s_a[:, :, :dec_seq]

    xp = x_prompt.reshape(batch * seq, d)
    xs = x_sample
    state_pool_tm = jnp.swapaxes(state_pool, 1, 2)
    v_rows, pool_p, pool_s, conv_p, conv_s = [], [], [], [], []
    for i in range(depth):
        j = i // 2
        nfin = norm_final if i == depth - 1 else None
        if i % 2 == 0:
            xp = _gmlp_prompt_call(xp, norm_mix, w_in_b, g_v_a, w_s_a, bias, w_out_b, layer=i, mixer=j)
            xs, v = _gmlp_sample_call(xs, norm_mix, w_in_b, g_v_a, ws4[j], b4[j], w_out_b, layer=i, mixer=j)
            v_rows.append(v)
        else:
            xp, pp = _pool_prompt_call(xp, norm_mix, w_pool_bf, scale_b, layer=i, mixer=j, seq=seq)
            xs, ps = _pool_sample_call(xs, state_pool_tm, norm_mix, w_pool_bf, scale_b, layer=i, mixer=j)
            pool_p.append(pp)
            pool_s.append(ps)
        xp, cp = _ffn_prompt_call(xp, norm_ffn, w_gate_b, w_val_b, conv_w, conv_b, w_down_b, nfin,
                                  layer=i, seq=seq)
        xs, cs = _ffn_sample_call(xs, state_ffn_conv, norm_ffn, w_gate_b, w_val_b, conv_w, conv_b,
                                  w_down_b, nfin, layer=i)
        conv_p.append(cp)
        conv_s.append(cs)
    return (xp.reshape(batch, seq, d), xs, jnp.stack(v_rows),
            jnp.stack(pool_p), jnp.swapaxes(jnp.stack(pool_s), 1, 2), jnp.stack(conv_p), jnp.stack(conv_s))
```

```python
import functools
import math

import jax
import jax.numpy as jnp
from jax import lax
from jax.experimental import pallas as pl
from jax.experimental.pallas import tpu as pltpu

EPS = 1e-6
CHUNK = 128
POOL_WINDOWS = (2, 4, 8, 16)
POOL_BUF = max(POOL_WINDOWS) - 1
CONV_W = 3
SUBLANES = 8
TOKEN_TILE = 512
FF_CHUNK = 512
GMLP_CHUNK = 1024
FFN_CAST_CHUNKS = 22
VMEM_LIMIT_BYTES = 56 * 1024 * 1024

_GELU_C = math.sqrt(2.0 / math.pi)


def _const_spec(shape):
    zeros = (0,) * len(shape)
    return pl.BlockSpec(shape, lambda i: zeros, pipeline_mode=pl.Buffered(1))


def _layer_spec(stacked_shape, layer):
    tail = (0,) * (len(stacked_shape) - 1)
    return pl.BlockSpec((None,) + tuple(stacked_shape[1:]), lambda i: (layer,) + tail,
                        pipeline_mode=pl.Buffered(1))


def _cast_job(src, layer, axis, n_chunks):
    rows, cols = src.shape[-2:]
    blk = (rows // n_chunks, cols) if axis == 0 else (rows, cols // n_chunks)
    assert blk[0] * (n_chunks if axis == 0 else 1) == rows and blk[1] * (n_chunks if axis == 1 else 1) == cols

    def chunk(i):
        c = jnp.minimum(i, n_chunks - 1)
        return (c, 0) if axis == 0 else (0, c)

    return (src, pl.BlockSpec((None,) + blk, lambda i: (layer,) + chunk(i)),
            pl.BlockSpec(blk, chunk), jax.ShapeDtypeStruct((rows, cols), jnp.bfloat16))


def _run_casts(cast_refs):
    n = len(cast_refs) // 2
    for src_ref, dst_ref in zip(cast_refs[:n], cast_refs[n:]):
        dst_ref[...] = src_ref[...].astype(jnp.bfloat16)


def _rms(x, g):
    ms = jnp.mean(x * x, axis=-1, keepdims=True)
    return x * lax.rsqrt(ms + EPS) * g


def _gelu_tanh(x):
    inner = x * (_GELU_C + (_GELU_C * 0.044715) * (x * x))
    return x * (0.5 + 0.5 * jnp.tanh(inner))


def _silu(x):
    return x * (1.0 / (1.0 + jnp.exp(-x)))


def _compiler_params():
    return pltpu.CompilerParams(dimension_semantics=("arbitrary",),
                                vmem_limit_bytes=VMEM_LIMIT_BYTES)


def _load_slabs(ref3d):
    return jnp.concatenate([ref3d[:, t, :] for t in range(ref3d.shape[1])], axis=0)


def _store_slabs(ref3d, val):
    n_seq = ref3d.shape[0]
    for t in range(ref3d.shape[1]):
        ref3d[:, t, :] = val[t * n_seq:(t + 1) * n_seq]


def _gmlp_tile(x, nm_ref, win_ref, gv_ref, wout_ref, v_ref, m_ref, gate_fn):
    d_v = wout_ref.shape[0]
    h = _rms(x, nm_ref[...]).astype(jnp.bfloat16)
    chunks = [(c0, min(c0 + GMLP_CHUNK, d_v)) for c0 in range(0, d_v, GMLP_CHUNK)]
    ssq = None
    for c0, c1 in chunks:
        zc = _gelu_tanh(jnp.dot(h, win_ref[:, d_v + c0:d_v + c1], preferred_element_type=jnp.float32))
        v_ref[:, c0:c1] = zc
        part = jnp.sum(zc * zc, axis=-1, keepdims=True)
        ssq = part if ssq is None else ssq + part
    rinv = lax.rsqrt(ssq * (1.0 / d_v) + EPS)
    for c0, c1 in chunks:
        uc = _gelu_tanh(jnp.dot(h, win_ref[:, c0:c1], preferred_element_type=jnp.float32))
        vc = v_ref[:, c0:c1] * rinv * gv_ref[:, c0:c1]
        gate_fn(uc, vc, c0, c1)
    y = jnp.dot(m_ref[...], wout_ref[...], preferred_element_type=jnp.float32)
    return x + y


def _gmlp_prompt_kernel(*refs, layer, mixer, n_cast):
    x_ref, nm_ref, win_ref, gv_ref, ws_ref, bias_ref, wout_ref = refs[:7]
    o_ref = refs[7 + n_cast]
    v_ref, m_ref = refs[-2:]
    _run_casts(refs[7:7 + n_cast] + refs[8 + n_cast:8 + 2 * n_cast])
    tm = x_ref.shape[0]
    hd = wout_ref.shape[0] // ws_ref.shape[0]
    row = lax.broadcasted_iota(jnp.int32, (CHUNK, CHUNK), 0)
    col = lax.broadcasted_iota(jnp.int32, (CHUNK, CHUNK), 1)

    def gate(uc, vc, c0, c1):
        vb = vc.astype(jnp.bfloat16)
        for hh in range(c0 // hd, c1 // hd):
            ws = jnp.where(row >= col, ws_ref[hh], 0.0).astype(jnp.bfloat16)
            l0, l1 = hh * hd - c0, (hh + 1) * hd - c0
            for c in range(tm // CHUNK):
                r0, r1 = c * CHUNK, (c + 1) * CHUNK
                s = jnp.dot(ws, vb[r0:r1, l0:l1], preferred_element_type=jnp.float32)
                s = s + bias_ref[:, c0 + l0:c0 + l1]
                m_ref[r0:r1, c0 + l0:c0 + l1] = (uc[r0:r1, l0:l1] * s).astype(jnp.bfloat16)

    o_ref[...] = _gmlp_tile(x_ref[...], nm_ref.at[layer:layer + 1, :], win_ref,
                            gv_ref.at[mixer:mixer + 1, :], wout_ref, v_ref, m_ref, gate)


def _gmlp_sample_kernel(xs_ref, nm_ref, win_ref, gv_ref, ws4_ref, b4_ref, wout_ref, os_ref, vs_ref,
                        v_ref, m_ref, *, layer, mixer):
    n_seq, seq_len, _ = xs_ref.shape
    hd = wout_ref.shape[0] // ws4_ref.shape[0]

    def gate(uc, vc, c0, c1):
        for t in range(seq_len):
            vs_ref[:, t, c0:c1] = vc[t * n_seq:(t + 1) * n_seq]
        for hh in range(c0 // hd, c1 // hd):
            l0, l1 = hh * hd - c0, (hh + 1) * hd - c0
            vj = [vc[j * n_seq:(j + 1) * n_seq, l0:l1] for j in range(seq_len)]
            for t in range(seq_len):
                s = b4_ref[hh, t] + ws4_ref[hh, t * seq_len] * vj[0]
                for j in range(1, t + 1):
                    s = s + ws4_ref[hh, t * seq_len + j] * vj[j]
                r0, r1 = t * n_seq, (t + 1) * n_seq
                m_ref[r0:r1, c0 + l0:c0 + l1] = (uc[r0:r1, l0:l1] * s).astype(jnp.bfloat16)

    out = _gmlp_tile(_load_slabs(xs_ref), nm_ref.at[layer:layer + 1, :], win_ref,
                     gv_ref.at[mixer:mixer + 1, :], wout_ref, v_ref, m_ref, gate)
    _store_slabs(os_ref, out)


def _gmlp_prompt_call(xp, norm_mix, w_in, g_v, w_s, bias, w_out, cast_jobs, *, layer, mixer):
    n, d = xp.shape
    d_v = w_out.shape[0]
    tm = TOKEN_TILE
    assert n % tm == 0 and tm % CHUNK == 0
    return pl.pallas_call(
        functools.partial(_gmlp_prompt_kernel, layer=layer, mixer=mixer, n_cast=len(cast_jobs)),
        grid=(n // tm,),
        in_specs=[
            pl.BlockSpec((tm, d), lambda i: (i, 0)),
            _const_spec(norm_mix.shape),
            _const_spec(w_in.shape),
            _const_spec(g_v.shape),
            _layer_spec(w_s.shape, mixer),
            _layer_spec(bias.shape, mixer),
            _const_spec(w_out.shape),
        ] + [job[1] for job in cast_jobs],
        out_specs=[pl.BlockSpec((tm, d), lambda i: (i, 0))] + [job[2] for job in cast_jobs],
        out_shape=[jax.ShapeDtypeStruct((n, d), jnp.float32)] + [job[3] for job in cast_jobs],
        scratch_shapes=[pltpu.VMEM((tm, d_v), jnp.float32), pltpu.VMEM((tm, d_v), jnp.bfloat16)],
        compiler_params=_compiler_params(),
        name="gmlp_mixer",
    )(xp, norm_mix, w_in, g_v, w_s, bias, w_out, *[job[0] for job in cast_jobs])


def _gmlp_sample_call(xs, norm_mix, w_in, g_v, ws4, b4, w_out, *, layer, mixer):
    n_seq, seq_len, d = xs.shape
    d_v = w_out.shape[0]
    assert n_seq % SUBLANES == 0
    smem = pl.BlockSpec(memory_space=pltpu.SMEM)
    v_shape = (n_seq, seq_len, d_v)
    return pl.pallas_call(
        functools.partial(_gmlp_sample_kernel, layer=layer, mixer=mixer),
        grid=(1,),
        in_specs=[
            _const_spec(xs.shape),
            _const_spec(norm_mix.shape),
            _const_spec(w_in.shape),
            _const_spec(g_v.shape),
            smem, smem,
            _const_spec(w_out.shape),
        ],
        out_specs=[pl.BlockSpec(xs.shape, lambda i: (0, 0, 0)), pl.BlockSpec(v_shape, lambda i: (0, 0, 0))],
        out_shape=[jax.ShapeDtypeStruct(xs.shape, jnp.float32), jax.ShapeDtypeStruct(v_shape, jnp.float32)],
        scratch_shapes=[pltpu.VMEM((n_seq * seq_len, d_v), jnp.float32),
                        pltpu.VMEM((n_seq * seq_len, d_v), jnp.bfloat16)],
        compiler_params=_compiler_params(),
        name="gmlp_mixer_sample",
    )(xs, norm_mix, w_in, g_v, ws4, b4, w_out)


def _ffn_tile(x, nf_ref, wg_ref, wv_ref, cw_ref, cb_ref, wd_ref, g_ref, history_fn, ng_ref):
    d_ff = wd_ref.shape[0]
    h = _rms(x, nf_ref[...]).astype(jnp.bfloat16)
    bounds = list(range(0, d_ff, FF_CHUNK)) + [d_ff]
    for c0, c1 in zip(bounds[:-1], bounds[1:]):
        a = jnp.dot(h, wg_ref[:, c0:c1], preferred_element_type=jnp.float32)
        val = jnp.dot(h, wv_ref[:, c0:c1], preferred_element_type=jnp.float32)
        prev1, prev2 = history_fn(a, c0, c1)
        conv = (cb_ref[:, c0:c1] + cw_ref[0:1, c0:c1] * prev2
                + cw_ref[1:2, c0:c1] * prev1 + cw_ref[2:3, c0:c1] * a)
        g_ref[:, c0:c1] = (_silu(conv) * val).astype(jnp.bfloat16)
    y = jnp.dot(g_ref[...], wd_ref[...], preferred_element_type=jnp.float32)
    out = x + y
    return out if ng_ref is None else _rms(out, ng_ref[...])


def _ffn_prompt_kernel(*refs, layer, tiles_per_seq, final, n_cast):
    it = iter(refs)
    x_ref, nf_ref, wg_ref, wv_ref, cw_ref, cbias_ref, wd_ref = (next(it) for _ in range(7))
    ng_ref = next(it) if final else None
    cast_refs = [next(it) for _ in range(n_cast)]
    o_ref, cp_ref = next(it), next(it)
    cast_refs += [next(it) for _ in range(n_cast)]
    g_ref, carry_ref = next(it), next(it)
    _run_casts(cast_refs)
    tm = x_ref.shape[0]
    keep = CONV_W - 1

    @pl.when(pl.program_id(0) % tiles_per_seq == 0)
    def _new_sequence():
        carry_ref[...] = jnp.zeros_like(carry_ref)

    def history(a, c0, c1):
        head = carry_ref[:, c0:c1]
        carry_ref[:, c0:c1] = a[tm - SUBLANES:, :]
        cp_ref[0, :, c0:c1] = a[tm - keep:, :]
        ext = jnp.concatenate([head, a], axis=0)
        return ext[SUBLANES - 1:SUBLANES - 1 + tm], ext[SUBLANES - 2:SUBLANES - 2 + tm]

    o_ref[...] = _ffn_tile(x_ref[...], nf_ref.at[layer:layer + 1, :], wg_ref, wv_ref, cw_ref,
                           cbias_ref.at[layer:layer + 1, :], wd_ref, g_ref, history, ng_ref)


def _ffn_sample_kernel(*refs, layer, final):
    it = iter(refs)
    xs_ref, cs_ref, nf_ref, wg_ref, wv_ref, cw_ref, cbias_ref, wd_ref = (next(it) for _ in range(8))
    ng_ref = next(it) if final else None
    os_ref, cso_ref, g_ref = (next(it) for _ in range(3))
    n_seq, seq_len, _ = xs_ref.shape
    rows = n_seq * seq_len
    keep = CONV_W - 1

    def history(a, c0, c1):
        old = [cs_ref[:, k, c0:c1] for k in range(keep)]
        for k in range(keep):
            t = seq_len - keep + k
            cso_ref[:, k, c0:c1] = a[t * n_seq:(t + 1) * n_seq]
        prev1 = jnp.concatenate([old[1], a[:rows - n_seq]], axis=0)
        prev2 = jnp.concatenate([old[0], old[1], a[:rows - 2 * n_seq]], axis=0)
        return prev1, prev2

    out = _ffn_tile(_load_slabs(xs_ref), nf_ref.at[layer:layer + 1, :], wg_ref, wv_ref, cw_ref,
                    cbias_ref.at[layer:layer + 1, :], wd_ref, g_ref, history, ng_ref)
    _store_slabs(os_ref, out)


def _ffn_weight_specs(norm_ffn, wg, wv, cw, cb, wd, norm_final, layer):
    args = [norm_ffn, wg, wv, cw, cb, wd]
    specs = [_const_spec(norm_ffn.shape), _const_spec(wg.shape), _const_spec(wv.shape),
             _layer_spec(cw.shape, layer), _const_spec(cb.shape), _const_spec(wd.shape)]
    if norm_final is not None:
        args.append(norm_final.reshape(1, -1))
        specs.append(_const_spec((1, norm_final.shape[0])))
    return args, specs


def _ffn_prompt_call(xp, norm_ffn, wg, wv, cw, cb, wd, norm_final, cast_jobs, *, layer, seq):
    n, d = xp.shape
    d_ff = wd.shape[0]
    tm = TOKEN_TILE
    assert n % tm == 0 and seq % tm == 0
    tiles_per_seq = seq // tm
    keep = CONV_W - 1
    w_args, w_specs = _ffn_weight_specs(norm_ffn, wg, wv, cw, cb, wd, norm_final, layer)
    return pl.pallas_call(
        functools.partial(_ffn_prompt_kernel, layer=layer, tiles_per_seq=tiles_per_seq,
                          final=norm_final is not None, n_cast=len(cast_jobs)),
        grid=(n // tm,),
        in_specs=[pl.BlockSpec((tm, d), lambda i: (i, 0))] + w_specs + [job[1] for job in cast_jobs],
        out_specs=[pl.BlockSpec((tm, d), lambda i: (i, 0)),
                   pl.BlockSpec((1, keep, d_ff), lambda i: (i // tiles_per_seq, 0, 0))]
        + [job[2] for job in cast_jobs],
        out_shape=[jax.ShapeDtypeStruct((n, d), jnp.float32),
                   jax.ShapeDtypeStruct((n // seq, keep, d_ff), jnp.float32)]
        + [job[3] for job in cast_jobs],
        scratch_shapes=[pltpu.VMEM((tm, d_ff), jnp.bfloat16),
                        pltpu.VMEM((SUBLANES, d_ff), jnp.float32)],
        compiler_params=_compiler_params(),
        name="conv_ffn",
    )(xp, *w_args, *[job[0] for job in cast_jobs])


def _ffn_sample_call(xs, conv_state, norm_ffn, wg, wv, cw, cb, wd, norm_final, *, layer):
    n_seq, seq_len, d = xs.shape
    d_ff = wd.shape[0]
    keep = CONV_W - 1
    assert seq_len >= keep and n_seq % SUBLANES == 0
    w_args, w_specs = _ffn_weight_specs(norm_ffn, wg, wv, cw, cb, wd, norm_final, layer)
    return pl.pallas_call(
        functools.partial(_ffn_sample_kernel, layer=layer, final=norm_final is not None),
        grid=(1,),
        in_specs=[_const_spec(xs.shape), _layer_spec(conv_state.shape, layer)] + w_specs,
        out_specs=[pl.BlockSpec(xs.shape, lambda i: (0, 0, 0)),
                   pl.BlockSpec((n_seq, keep, d_ff), lambda i: (0, 0, 0))],
        out_shape=[jax.ShapeDtypeStruct(xs.shape, jnp.float32),
                   jax.ShapeDtypeStruct((n_seq, keep, d_ff), jnp.float32)],
        scratch_shapes=[pltpu.VMEM((n_seq * seq_len, d_ff), jnp.bfloat16)],
        compiler_params=_compiler_params(),
        name="conv_ffn_sample",
    )(xs, conv_state, *w_args)


def _pool_project(p_groups, wp_ref, scale_ref):
    gd = wp_ref.shape[1]
    ys = [jnp.dot(p.astype(jnp.bfloat16), wp_ref[g], preferred_element_type=jnp.float32)
          * scale_ref[:, g * gd:(g + 1) * gd] for g, p in enumerate(p_groups)]
    return jnp.concatenate(ys, axis=-1)


def _pool_prompt_kernel(x_ref, nm_ref, wp_ref, scale_ref, o_ref, pp_ref, carry_ref,
                        *, layer, mixer, tiles_per_seq):
    gd = wp_ref.shape[1]
    tm = x_ref.shape[0]
    halo = POOL_BUF + 1
    x = x_ref[...]
    h = _rms(x, nm_ref[layer:layer + 1, :])
    tile_in_seq = pl.program_id(0) % tiles_per_seq

    @pl.when(tile_in_seq == 0)
    def _new_sequence():
        carry_ref[...] = jnp.zeros_like(carry_ref)

    prev = carry_ref[...]
    carry_ref[...] = h[tm - halo:, :]
    pp_ref[0] = h[tm - POOL_BUF:, :]
    pos = tile_in_seq * tm + lax.broadcasted_iota(jnp.int32, (tm, gd), 0)
    p_groups = []
    for g, w in enumerate(POOL_WINDOWS):
        c0, c1 = g * gd, (g + 1) * gd
        hg = h[:, c0:c1]
        acc = jnp.concatenate([prev[:, c0:c1], hg], axis=0)
        span = 1
        while span < w:
            acc = acc[span:] + acc[:-span]
            span *= 2
        total = acc[halo - (w - 1):]
        cnt = jnp.minimum(w, pos + 1).astype(jnp.float32)
        p_groups.append(total / cnt - hg)
    o_ref[...] = x + _pool_project(p_groups, wp_ref, scale_ref.at[mixer:mixer + 1, :])


def _pool_sample_kernel(xs_ref, ps_ref, nm_ref, wp_ref, scale_ref, os_ref, pso_ref, hs_ref, *, layer, mixer):
    gd = wp_ref.shape[1]
    n_seq, seq_len, _ = xs_ref.shape
    for t in range(seq_len):
        ht = _rms(xs_ref[:, t, :], nm_ref[layer:layer + 1, :])
        hs_ref[t] = ht
        pso_ref[POOL_BUF - seq_len + t] = ht
    for k in range(POOL_BUF - seq_len):
        pso_ref[k] = ps_ref[k + seq_len]

    def slab(k, c0, c1):
        return ps_ref[k, :, c0:c1] if k < POOL_BUF else hs_ref[k - POOL_BUF, :, c0:c1]

    for t in range(seq_len):
        last = POOL_BUF + t
        p_groups = []
        for g, w in enumerate(POOL_WINDOWS):
            c0, c1 = g * gd, (g + 1) * gd
            total = slab(last, c0, c1)
            for k in range(last - w + 1, last):
                total = total + slab(k, c0, c1)
            p_groups.append(total / float(w) - slab(last, c0, c1))
        os_ref[:, t, :] = xs_ref[:, t, :] + _pool_project(p_groups, wp_ref, scale_ref.at[mixer:mixer + 1, :])


def _pool_prompt_call(xp, norm_mix, wp, scale, *, layer, mixer, seq):
    n, d = xp.shape
    tm = TOKEN_TILE
    assert n % tm == 0 and seq % tm == 0
    tiles_per_seq = seq // tm
    return pl.pallas_call(
        functools.partial(_pool_prompt_kernel, layer=layer, mixer=mixer, tiles_per_seq=tiles_per_seq),
        grid=(n // tm,),
        in_specs=[
            pl.BlockSpec((tm, d), lambda i: (i, 0)),
            _const_spec(norm_mix.shape),
            _const_spec(wp.shape),
            _const_spec(scale.shape),
        ],
        out_specs=[pl.BlockSpec((tm, d), lambda i: (i, 0)),
                   pl.BlockSpec((1, POOL_BUF, d), lambda i: (i // tiles_per_seq, 0, 0))],
        out_shape=[jax.ShapeDtypeStruct((n, d), jnp.float32),
                   jax.ShapeDtypeStruct((n // seq, POOL_BUF, d), jnp.float32)],
        scratch_shapes=[pltpu.VMEM((POOL_BUF + 1, d), jnp.float32)],
        compiler_params=_compiler_params(),
        name="pool_mixer",
    )(xp, norm_mix, wp, scale)


def _pool_sample_call(xs, pool_state_tm, norm_mix, wp, scale, *, layer, mixer):
    n_seq, seq_len, d = xs.shape
    assert seq_len <= POOL_BUF and n_seq % SUBLANES == 0
    return pl.pallas_call(
        functools.partial(_pool_sample_kernel, layer=layer, mixer=mixer),
        grid=(1,),
        in_specs=[
            _const_spec(xs.shape),
            _layer_spec(pool_state_tm.shape, mixer),
            _const_spec(norm_mix.shape),
            _const_spec(wp.shape),
            _const_spec(scale.shape),
        ],
        out_specs=[pl.BlockSpec(xs.shape, lambda i: (0, 0, 0)),
                   pl.BlockSpec((POOL_BUF, n_seq, d), lambda i: (0, 0, 0))],
        out_shape=[jax.ShapeDtypeStruct(xs.shape, jnp.float32),
                   jax.ShapeDtypeStruct((POOL_BUF, n_seq, d), jnp.float32)],
        scratch_shapes=[pltpu.VMEM((seq_len, n_seq, d), jnp.float32)],
        compiler_params=_compiler_params(),
        name="pool_mixer_sample",
    )(xs, pool_state_tm, norm_mix, wp, scale)


def kernel(x_prompt, x_sample, state_pool, state_ffn_conv, norm_mix, norm_ffn, norm_final,
           w_in_a, g_v_a, w_s_a, b_s_a, w_out_a, w_pool_b, scale_b,
           w_gate, w_val, conv_w, conv_b, w_down):
    batch, seq, d = x_prompt.shape
    dec_batch, dec_seq, _ = x_sample.shape
    depth = w_gate.shape[0]
    n_heads = w_s_a.shape[1]
    d_v = w_out_a.shape[1]
    hd = d_v // n_heads
    bf16 = jnp.bfloat16

    n_steps = (batch * seq) // TOKEN_TILE
    gd = w_pool_b.shape[-1]
    w_pool_2d = w_pool_b.reshape(w_pool_b.shape[0], -1, gd)

    def ffn_jobs(layer):
        return [_cast_job(w_gate, layer, 1, FFN_CAST_CHUNKS), _cast_job(w_val, layer, 1, FFN_CAST_CHUNKS),
                _cast_job(w_down, layer, 0, FFN_CAST_CHUNKS)]

    def gmlp_jobs(mixer):
        return [_cast_job(w_in_a, mixer, 1, n_steps), _cast_job(w_out_a, mixer, 0, n_steps)]

    assert FFN_CAST_CHUNKS <= n_steps
    gmlp_w = (w_in_a[0].astype(bf16), w_out_a[0].astype(bf16))

    bias = jnp.repeat(jnp.swapaxes(b_s_a, 1, 2), hd, axis=2)
    ws4 = w_s_a[:, :, :dec_seq, :dec_seq].reshape(-1, n_heads, dec_seq * dec_seq)
    b4 = b_s_a[:, :, :dec_seq]

    xp = x_prompt.reshape(batch * seq, d)
    xs = x_sample
    state_pool_tm = jnp.swapaxes(state_pool, 1, 2)
    v_rows, pool_p, pool_s, conv_p, conv_s = [], [], [], [], []
    for i in range(depth):
        j = i // 2
        nfin = norm_final if i == depth - 1 else None
        if i % 2 == 0:
            xp, *ffn_w = _gmlp_prompt_call(xp, norm_mix, gmlp_w[0], g_v_a, w_s_a, bias, gmlp_w[1],
                                           ffn_jobs(i), layer=i, mixer=j)
            xs, v = _gmlp_sample_call(xs, norm_mix, gmlp_w[0], g_v_a, ws4[j], b4[j], gmlp_w[1],
                                      layer=i, mixer=j)
            v_rows.append(v)
        else:
            pool_w = pool_w.reshape(-1, gd, gd)
            xp, pp = _pool_prompt_call(xp, norm_mix, pool_w, scale_b, layer=i, mixer=j, seq=seq)
            xs, ps = _pool_sample_call(xs, state_pool_tm, norm_mix, pool_w, scale_b, layer=i, mixer=j)
            pool_p.append(pp)
            pool_s.append(ps)
        if i == depth - 1:
            jobs = []
        elif i % 2 == 0:
            jobs = [_cast_job(w_pool_2d, j, 0, n_steps)] + ffn_jobs(i + 1)
        else:
            jobs = gmlp_jobs(j + 1)
        xp, cp, *next_w = _ffn_prompt_call(xp, norm_ffn, ffn_w[0], ffn_w[1], conv_w, conv_b, ffn_w[2], nfin,
                                           jobs, layer=i, seq=seq)
        xs, cs = _ffn_sample_call(xs, state_ffn_conv, norm_ffn, ffn_w[0], ffn_w[1], conv_w, conv_b,
                                  ffn_w[2], nfin, layer=i)
        conv_p.append(cp)
        conv_s.append(cs)
        if i % 2 == 0 and i < depth - 1:
            pool_w, ffn_w = next_w[0], next_w[1:]
        elif i < depth - 1:
            gmlp_w = next_w
    return (xp.reshape(batch, seq, d), xs, jnp.stack(v_rows),
            jnp.stack(pool_p), jnp.swapaxes(jnp.stack(pool_s), 1, 2), jnp.stack(conv_p), jnp.stack(conv_s))
```

```python
import functools
import math

import jax
import jax.numpy as jnp
from jax import lax
from jax.experimental import pallas as pl
from jax.experimental.pallas import tpu as pltpu

EPS = 1e-6
CHUNK = 128
POOL_WINDOWS = (2, 4, 8, 16)
POOL_BUF = max(POOL_WINDOWS) - 1
CONV_W = 3
SUBLANES = 8
TOKEN_TILE = 512
FF_CHUNK = 512
GMLP_CHUNK = 1024
FFN_CAST_CHUNKS = 22
OUT_CHUNKS = 4
STAGE_BLOCKS = OUT_CHUNKS
VMEM_LIMIT_BYTES = 56 * 1024 * 1024

_GELU_C = math.sqrt(2.0 / math.pi)


def _const_spec(shape):
    zeros = (0,) * len(shape)
    return pl.BlockSpec(shape, lambda i: zeros, pipeline_mode=pl.Buffered(1))


def _layer_spec(stacked_shape, layer):
    tail = (0,) * (len(stacked_shape) - 1)
    return pl.BlockSpec((None,) + tuple(stacked_shape[1:]), lambda i: (layer,) + tail,
                        pipeline_mode=pl.Buffered(1))


def _cast_job(src, layer, axis, n_chunks):
    rows, cols = src.shape[-2:]
    blk = (rows // n_chunks, cols) if axis == 0 else (rows, cols // n_chunks)
    assert blk[0] * (n_chunks if axis == 0 else 1) == rows and blk[1] * (n_chunks if axis == 1 else 1) == cols

    def chunk(i):
        c = jnp.minimum(i, n_chunks - 1)
        return (c, 0) if axis == 0 else (0, c)

    return (src, pl.BlockSpec((None,) + blk, lambda i: (layer,) + chunk(i)),
            pl.BlockSpec(blk, chunk), jax.ShapeDtypeStruct((rows, cols), jnp.bfloat16))


def _run_casts(cast_refs):
    n = len(cast_refs) // 2
    for src_ref, dst_ref in zip(cast_refs[:n], cast_refs[n:]):
        dst_ref[...] = src_ref[...].astype(jnp.bfloat16)


def _rms(x, g):
    ms = jnp.mean(x * x, axis=-1, keepdims=True)
    return x * lax.rsqrt(ms + EPS) * g


def _gelu_tanh(x):
    inner = x * (_GELU_C + (_GELU_C * 0.044715) * (x * x))
    return x * (0.5 + 0.5 * jnp.tanh(inner))


def _silu(x):
    return x * (1.0 / (1.0 + jnp.exp(-x)))


def _compiler_params():
    return pltpu.CompilerParams(dimension_semantics=("arbitrary",),
                                vmem_limit_bytes=VMEM_LIMIT_BYTES)


def _load_slabs(ref3d):
    return jnp.concatenate([ref3d[:, t, :] for t in range(ref3d.shape[1])], axis=0)


def _store_slabs(ref3d, val):
    n_seq = ref3d.shape[0]
    for t in range(ref3d.shape[1]):
        ref3d[:, t, :] = val[t * n_seq:(t + 1) * n_seq]


def _gmlp_tile(x, nm_ref, win_ref, gv_ref, wout_ref, v_ref, m_ref, gate_fn):
    d_v = wout_ref.shape[0]
    h = _rms(x, nm_ref[...]).astype(jnp.bfloat16)
    chunks = [(c0, min(c0 + GMLP_CHUNK, d_v)) for c0 in range(0, d_v, GMLP_CHUNK)]
    ssq = None
    for c0, c1 in chunks:
        zc = _gelu_tanh(jnp.dot(h, win_ref[:, d_v + c0:d_v + c1], preferred_element_type=jnp.float32))
        v_ref[:, c0:c1] = zc
        part = jnp.sum(zc * zc, axis=-1, keepdims=True)
        ssq = part if ssq is None else ssq + part
    rinv = lax.rsqrt(ssq * (1.0 / d_v) + EPS)
    for c0, c1 in chunks:
        uc = _gelu_tanh(jnp.dot(h, win_ref[:, c0:c1], preferred_element_type=jnp.float32))
        vc = v_ref[:, c0:c1] * rinv * gv_ref[:, c0:c1]
        gate_fn(uc, vc, c0, c1)
    y = jnp.dot(m_ref[...], wout_ref[...], preferred_element_type=jnp.float32)
    return x + y


def _gmlp_prompt_kernel(*refs, layer, mixer, n_cast):
    x_ref, nm_ref, win_ref, gv_ref, ws_ref, bias_ref, wout_ref = refs[:7]
    o_ref = refs[7 + n_cast]
    v_ref, m_ref = refs[-2:]
    _run_casts(refs[7:7 + n_cast] + refs[8 + n_cast:8 + 2 * n_cast])
    tm = x_ref.shape[0]
    hd = wout_ref.shape[0] // ws_ref.shape[0]
    row = lax.broadcasted_iota(jnp.int32, (CHUNK, CHUNK), 0)
    col = lax.broadcasted_iota(jnp.int32, (CHUNK, CHUNK), 1)

    def gate(uc, vc, c0, c1):
        vb = vc.astype(jnp.bfloat16)
        for hh in range(c0 // hd, c1 // hd):
            ws = jnp.where(row >= col, ws_ref[hh], 0.0).astype(jnp.bfloat16)
            l0, l1 = hh * hd - c0, (hh + 1) * hd - c0
            for c in range(tm // CHUNK):
                r0, r1 = c * CHUNK, (c + 1) * CHUNK
                s = jnp.dot(ws, vb[r0:r1, l0:l1], preferred_element_type=jnp.float32)
                s = s + bias_ref[:, c0 + l0:c0 + l1]
                m_ref[r0:r1, c0 + l0:c0 + l1] = (uc[r0:r1, l0:l1] * s).astype(jnp.bfloat16)

    o_ref[...] = _gmlp_tile(x_ref[...], nm_ref.at[layer:layer + 1, :], win_ref,
                            gv_ref.at[mixer:mixer + 1, :], wout_ref, v_ref, m_ref, gate)


def _gmlp_sample_kernel(xs_ref, nm_ref, win_ref, gv_ref, ws4_ref, b4_ref, wout_ref, os_ref, vs_ref,
                        v_ref, m_ref, *, layer, mixer):
    n_seq, seq_len, _ = xs_ref.shape
    hd = wout_ref.shape[0] // ws4_ref.shape[0]

    def gate(uc, vc, c0, c1):
        for t in range(seq_len):
            vs_ref[:, t, c0:c1] = vc[t * n_seq:(t + 1) * n_seq]
        for hh in range(c0 // hd, c1 // hd):
            l0, l1 = hh * hd - c0, (hh + 1) * hd - c0
            vj = [vc[j * n_seq:(j + 1) * n_seq, l0:l1] for j in range(seq_len)]
            for t in range(seq_len):
                s = b4_ref[hh, t] + ws4_ref[hh, t * seq_len] * vj[0]
                for j in range(1, t + 1):
                    s = s + ws4_ref[hh, t * seq_len + j] * vj[j]
                r0, r1 = t * n_seq, (t + 1) * n_seq
                m_ref[r0:r1, c0 + l0:c0 + l1] = (uc[r0:r1, l0:l1] * s).astype(jnp.bfloat16)

    out = _gmlp_tile(_load_slabs(xs_ref), nm_ref.at[layer:layer + 1, :], win_ref,
                     gv_ref.at[mixer:mixer + 1, :], wout_ref, v_ref, m_ref, gate)
    _store_slabs(os_ref, out)


def _gmlp_prompt_call(xp, norm_mix, w_in, g_v, w_s, bias, w_out, cast_jobs, *, layer, mixer):
    n, d = xp.shape
    d_v = w_out.shape[0]
    tm = TOKEN_TILE
    assert n % tm == 0 and tm % CHUNK == 0
    return pl.pallas_call(
        functools.partial(_gmlp_prompt_kernel, layer=layer, mixer=mixer, n_cast=len(cast_jobs)),
        grid=(n // tm,),
        in_specs=[
            pl.BlockSpec((tm, d), lambda i: (i, 0)),
            _const_spec(norm_mix.shape),
            _const_spec(w_in.shape),
            _const_spec(g_v.shape),
            _layer_spec(w_s.shape, mixer),
            _layer_spec(bias.shape, mixer),
            _const_spec(w_out.shape),
        ] + [job[1] for job in cast_jobs],
        out_specs=[pl.BlockSpec((tm, d), lambda i: (i, 0))] + [job[2] for job in cast_jobs],
        out_shape=[jax.ShapeDtypeStruct((n, d), jnp.float32)] + [job[3] for job in cast_jobs],
        scratch_shapes=[pltpu.VMEM((tm, d_v), jnp.float32), pltpu.VMEM((tm, d_v), jnp.bfloat16)],
        compiler_params=_compiler_params(),
        name="gmlp_mixer",
    )(xp, norm_mix, w_in, g_v, w_s, bias, w_out, *[job[0] for job in cast_jobs])


def _gmlp_sample_call(xs, norm_mix, w_in, g_v, ws4, b4, w_out, *, layer, mixer):
    n_seq, seq_len, d = xs.shape
    d_v = w_out.shape[0]
    assert n_seq % SUBLANES == 0
    smem = pl.BlockSpec(memory_space=pltpu.SMEM)
    v_shape = (n_seq, seq_len, d_v)
    return pl.pallas_call(
        functools.partial(_gmlp_sample_kernel, layer=layer, mixer=mixer),
        grid=(1,),
        in_specs=[
            _const_spec(xs.shape),
            _const_spec(norm_mix.shape),
            _const_spec(w_in.shape),
            _const_spec(g_v.shape),
            smem, smem,
            _const_spec(w_out.shape),
        ],
        out_specs=[pl.BlockSpec(xs.shape, lambda i: (0, 0, 0)), pl.BlockSpec(v_shape, lambda i: (0, 0, 0))],
        out_shape=[jax.ShapeDtypeStruct(xs.shape, jnp.float32), jax.ShapeDtypeStruct(v_shape, jnp.float32)],
        scratch_shapes=[pltpu.VMEM((n_seq * seq_len, d_v), jnp.float32),
                        pltpu.VMEM((n_seq * seq_len, d_v), jnp.bfloat16)],
        compiler_params=_compiler_params(),
        name="gmlp_mixer_sample",
    )(xs, norm_mix, w_in, g_v, ws4, b4, w_out)


def _ffn_tile(h, wg_ref, wv_ref, cw_ref, cb_ref, wd_ref, g_ref, history_fn, emit_fn, n_out_chunks=1):
    d_ff, d = wd_ref.shape
    bounds = list(range(0, d_ff, FF_CHUNK)) + [d_ff]
    for c0, c1 in zip(bounds[:-1], bounds[1:]):
        a = jnp.dot(h, wg_ref[:, c0:c1], preferred_element_type=jnp.float32)
        val = jnp.dot(h, wv_ref[:, c0:c1], preferred_element_type=jnp.float32)
        prev1, prev2 = history_fn(a, c0, c1)
        conv = (cb_ref[:, c0:c1] + cw_ref[0:1, c0:c1] * prev2
                + cw_ref[1:2, c0:c1] * prev1 + cw_ref[2:3, c0:c1] * a)
        g_ref[:, c0:c1] = (_silu(conv) * val).astype(jnp.bfloat16)
    step = d // n_out_chunks
    for k in range(n_out_chunks):
        n0, n1 = k * step, (k + 1) * step
        emit_fn(k, n0, n1, jnp.dot(g_ref[...], wd_ref[:, n0:n1], preferred_element_type=jnp.float32))


def _pool_mix(x, h, prev, first_pos, wp_ref, scale_ref):
    rows = x.shape[0]
    gd = wp_ref.shape[1]
    halo = POOL_BUF + 1
    pos = first_pos + lax.broadcasted_iota(jnp.int32, (rows, gd), 0)
    p_groups = []
    for g, w in enumerate(POOL_WINDOWS):
        c0, c1 = g * gd, (g + 1) * gd
        hg = h[:, c0:c1]
        acc = jnp.concatenate([prev[:, c0:c1], hg], axis=0)
        span = 1
        while span < w:
            acc = acc[span:] + acc[:-span]
            span *= 2
        total = acc[halo - (w - 1):]
        cnt = jnp.minimum(w, pos + 1).astype(jnp.float32)
        p_groups.append(total / cnt - hg)
    return x + _pool_project(p_groups, wp_ref, scale_ref)


def _ffn_prompt_kernel(*refs, layer, tiles_per_seq, final, n_cast, pool_mixer):
    pool = pool_mixer is not None
    it = iter(refs)
    x_ref, xn_ref, nf_ref, wg_ref, wv_ref, cw_ref, cbias_ref, wd_ref = (next(it) for _ in range(8))
    ng_ref = next(it) if final else None
    nm_ref, wp_ref, scale_ref = (next(it), next(it), next(it)) if pool else (None, None, None)
    cast_refs = [next(it) for _ in range(n_cast)]
    o_ref, cp_ref = next(it), next(it)
    pp_ref = next(it) if pool else None
    cast_refs += [next(it) for _ in range(n_cast)]
    g_ref, carry_ref, h_ref = (next(it) for _ in range(3))
    x1_ref, pcarry_ref = (next(it), next(it)) if pool else (None, None)
    i = pl.program_id(0)
    tm = xn_ref.shape[0]
    rb = tm // STAGE_BLOCKS
    keep = CONV_W - 1
    halo = POOL_BUF + 1

    def stage_block(src_ref, r, tile_in_seq):
        r0, r1 = r * rb, (r + 1) * rb
        x = src_ref[r0:r1, :]
        if pool:
            h1 = _rms(x, nm_ref[layer:layer + 1, :])
            prev = pcarry_ref[...]
            pcarry_ref[...] = h1[rb - halo:, :]
            if r == STAGE_BLOCKS - 1:
                pp_ref[0] = h1[rb - POOL_BUF:, :]
            x = _pool_mix(x, h1, prev, tile_in_seq * tm + r0, wp_ref,
                          scale_ref.at[pool_mixer:pool_mixer + 1, :])
            x1_ref[r0:r1, :] = x
        h_ref[r0:r1, :] = _rms(x, nf_ref[layer:layer + 1, :]).astype(jnp.bfloat16)

    @pl.when(i == 0)
    def _first_tile():
        if pool:
            pcarry_ref[...] = jnp.zeros_like(pcarry_ref)
        for r in range(STAGE_BLOCKS):
            stage_block(x_ref, r, 0)

    @pl.when(i % tiles_per_seq == 0)
    def _new_sequence():
        carry_ref[...] = jnp.zeros_like(carry_ref)

    if pool:
        @pl.when((i + 1) % tiles_per_seq == 0)
        def _next_tile_starts_sequence():
            pcarry_ref[...] = jnp.zeros_like(pcarry_ref)

    _run_casts(cast_refs)

    def history(a, c0, c1):
        head = carry_ref[:, c0:c1]
        carry_ref[:, c0:c1] = a[tm - SUBLANES:, :]
        cp_ref[0, :, c0:c1] = a[tm - keep:, :]
        ext = jnp.concatenate([head, a], axis=0)
        return ext[SUBLANES - 1:SUBLANES - 1 + tm], ext[SUBLANES - 2:SUBLANES - 2 + tm]

    def emit(k, n0, n1, y):
        if pool:
            if k == 0:
                o_ref[...] = x1_ref[...]
            o_ref[:, n0:n1] = o_ref[:, n0:n1] + y
        else:
            o_ref[:, n0:n1] = x_ref[:, n0:n1] + y
        stage_block(xn_ref, k, (i + 1) % tiles_per_seq)

    _ffn_tile(h_ref[...], wg_ref, wv_ref, cw_ref, cbias_ref.at[layer:layer + 1, :], wd_ref, g_ref,
              history, emit, OUT_CHUNKS)
    if final:
        o_ref[...] = _rms(o_ref[...], ng_ref[...])


def _ffn_sample_kernel(*refs, layer, final):
    it = iter(refs)
    xs_ref, cs_ref, nf_ref, wg_ref, wv_ref, cw_ref, cbias_ref, wd_ref = (next(it) for _ in range(8))
    ng_ref = next(it) if final else None
    os_ref, cso_ref, g_ref = (next(it) for _ in range(3))
    n_seq, seq_len, _ = xs_ref.shape
    rows = n_seq * seq_len
    keep = CONV_W - 1

    def history(a, c0, c1):
        old = [cs_ref[:, k, c0:c1] for k in range(keep)]
        for k in range(keep):
            t = seq_len - keep + k
            cso_ref[:, k, c0:c1] = a[t * n_seq:(t + 1) * n_seq]
        prev1 = jnp.concatenate([old[1], a[:rows - n_seq]], axis=0)
        prev2 = jnp.concatenate([old[0], old[1], a[:rows - 2 * n_seq]], axis=0)
        return prev1, prev2

    x = _load_slabs(xs_ref)

    def emit(k, n0, n1, y):
        out = x + y
        _store_slabs(os_ref, out if ng_ref is None else _rms(out, ng_ref[...]))

    _ffn_tile(_rms(x, nf_ref[layer:layer + 1, :]).astype(jnp.bfloat16), wg_ref, wv_ref, cw_ref,
              cbias_ref.at[layer:layer + 1, :], wd_ref, g_ref, history, emit)


def _ffn_weight_specs(norm_ffn, wg, wv, cw, cb, wd, norm_final, layer):
    args = [norm_ffn, wg, wv, cw, cb, wd]
    specs = [_const_spec(norm_ffn.shape), _const_spec(wg.shape), _const_spec(wv.shape),
             _layer_spec(cw.shape, layer), _const_spec(cb.shape), _const_spec(wd.shape)]
    if norm_final is not None:
        args.append(norm_final.reshape(1, -1))
        specs.append(_const_spec((1, norm_final.shape[0])))
    return args, specs


def _ffn_prompt_call(xp, norm_ffn, wg, wv, cw, cb, wd, norm_final, cast_jobs, *, layer, seq, pool=None):
    n, d = xp.shape
    d_ff = wd.shape[0]
    tm = TOKEN_TILE
    assert n % tm == 0 and seq % tm == 0 and (tm // STAGE_BLOCKS) % (2 * SUBLANES) == 0
    assert tm // STAGE_BLOCKS >= POOL_BUF + 1 and d % OUT_CHUNKS == 0
    n_tiles, tiles_per_seq = n // tm, seq // tm
    keep = CONV_W - 1
    w_args, w_specs = _ffn_weight_specs(norm_ffn, wg, wv, cw, cb, wd, norm_final, layer)
    next_tile = pl.BlockSpec((tm, d), lambda i: (jnp.minimum(i + 1, n_tiles - 1), 0))
    in_specs = [_const_spec((tm, d)) if pool else pl.BlockSpec((tm, d), lambda i: (i, 0)), next_tile] + w_specs
    args = [xp, xp] + w_args
    out_specs = [pl.BlockSpec((tm, d), lambda i: (i, 0)),
                 pl.BlockSpec((1, keep, d_ff), lambda i: (i // tiles_per_seq, 0, 0))]
    out_shape = [jax.ShapeDtypeStruct((n, d), jnp.float32),
                 jax.ShapeDtypeStruct((n // seq, keep, d_ff), jnp.float32)]
    scratch = [pltpu.VMEM((tm, d_ff), jnp.bfloat16), pltpu.VMEM((SUBLANES, d_ff), jnp.float32),
               pltpu.VMEM((tm, d), jnp.bfloat16)]
    if pool:
        norm_mix, wp, scale, mixer = pool
        args += [norm_mix, wp, scale]
        in_specs += [_const_spec(norm_mix.shape), _const_spec(wp.shape), _const_spec(scale.shape)]
        out_specs.append(pl.BlockSpec(
            (1, POOL_BUF, d), lambda i: (jnp.minimum(i + 1, n_tiles - 1) // tiles_per_seq, 0, 0)))
        out_shape.append(jax.ShapeDtypeStruct((n // seq, POOL_BUF, d), jnp.float32))
        scratch += [pltpu.VMEM((tm, d), jnp.float32), pltpu.VMEM((POOL_BUF + 1, d), jnp.float32)]
    return pl.pallas_call(
        functools.partial(_ffn_prompt_kernel, layer=layer, tiles_per_seq=tiles_per_seq,
                          final=norm_final is not None, n_cast=len(cast_jobs),
                          pool_mixer=pool[3] if pool else None),
        grid=(n_tiles,),
        in_specs=in_specs + [job[1] for job in cast_jobs],
        out_specs=out_specs + [job[2] for job in cast_jobs],
        out_shape=out_shape + [job[3] for job in cast_jobs],
        scratch_shapes=scratch,
        compiler_params=_compiler_params(),
        name="conv_ffn",
    )(*args, *[job[0] for job in cast_jobs])


def _ffn_sample_call(xs, conv_state, norm_ffn, wg, wv, cw, cb, wd, norm_final, *, layer):
    n_seq, seq_len, d = xs.shape
    d_ff = wd.shape[0]
    keep = CONV_W - 1
    assert seq_len >= keep and n_seq % SUBLANES == 0
    w_args, w_specs = _ffn_weight_specs(norm_ffn, wg, wv, cw, cb, wd, norm_final, layer)
    return pl.pallas_call(
        functools.partial(_ffn_sample_kernel, layer=layer, final=norm_final is not None),
        grid=(1,),
        in_specs=[_const_spec(xs.shape), _layer_spec(conv_state.shape, layer)] + w_specs,
        out_specs=[pl.BlockSpec(xs.shape, lambda i: (0, 0, 0)),
                   pl.BlockSpec((n_seq, keep, d_ff), lambda i: (0, 0, 0))],
        out_shape=[jax.ShapeDtypeStruct(xs.shape, jnp.float32),
                   jax.ShapeDtypeStruct((n_seq, keep, d_ff), jnp.float32)],
        scratch_shapes=[pltpu.VMEM((n_seq * seq_len, d_ff), jnp.bfloat16)],
        compiler_params=_compiler_params(),
        name="conv_ffn_sample",
    )(xs, conv_state, *w_args)


def _pool_project(p_groups, wp_ref, scale_ref):
    gd = wp_ref.shape[1]
    ys = [jnp.dot(p.astype(jnp.bfloat16), wp_ref[g], preferred_element_type=jnp.float32)
          * scale_ref[:, g * gd:(g + 1) * gd] for g, p in enumerate(p_groups)]
    return jnp.concatenate(ys, axis=-1)


def _pool_sample_kernel(xs_ref, ps_ref, nm_ref, wp_ref, scale_ref, os_ref, pso_ref, hs_ref, *, layer, mixer):
    gd = wp_ref.shape[1]
    n_seq, seq_len, _ = xs_ref.shape
    for t in range(seq_len):
        ht = _rms(xs_ref[:, t, :], nm_ref[layer:layer + 1, :])
        hs_ref[t] = ht
        pso_ref[POOL_BUF - seq_len + t] = ht
    for k in range(POOL_BUF - seq_len):
        pso_ref[k] = ps_ref[k + seq_len]

    def slab(k, c0, c1):
        return ps_ref[k, :, c0:c1] if k < POOL_BUF else hs_ref[k - POOL_BUF, :, c0:c1]

    for t in range(seq_len):
        last = POOL_BUF + t
        p_groups = []
        for g, w in enumerate(POOL_WINDOWS):
            c0, c1 = g * gd, (g + 1) * gd
            total = slab(last, c0, c1)
            for k in range(last - w + 1, last):
                total = total + slab(k, c0, c1)
            p_groups.append(total / float(w) - slab(last, c0, c1))
        os_ref[:, t, :] = xs_ref[:, t, :] + _pool_project(p_groups, wp_ref, scale_ref.at[mixer:mixer + 1, :])


def _pool_sample_call(xs, pool_state_tm, norm_mix, wp, scale, *, layer, mixer):
    n_seq, seq_len, d = xs.shape
    assert seq_len <= POOL_BUF and n_seq % SUBLANES == 0
    return pl.pallas_call(
        functools.partial(_pool_sample_kernel, layer=layer, mixer=mixer),
        grid=(1,),
        in_specs=[
            _const_spec(xs.shape),
            _layer_spec(pool_state_tm.shape, mixer),
            _const_spec(norm_mix.shape),
            _const_spec(wp.shape),
            _const_spec(scale.shape),
        ],
        out_specs=[pl.BlockSpec(xs.shape, lambda i: (0, 0, 0)),
                   pl.BlockSpec((POOL_BUF, n_seq, d), lambda i: (0, 0, 0))],
        out_shape=[jax.ShapeDtypeStruct(xs.shape, jnp.float32),
                   jax.ShapeDtypeStruct((POOL_BUF, n_seq, d), jnp.float32)],
        scratch_shapes=[pltpu.VMEM((seq_len, n_seq, d), jnp.float32)],
        compiler_params=_compiler_params(),
        name="pool_mixer_sample",
    )(xs, pool_state_tm, norm_mix, wp, scale)


def kernel(x_prompt, x_sample, state_pool, state_ffn_conv, norm_mix, norm_ffn, norm_final,
           w_in_a, g_v_a, w_s_a, b_s_a, w_out_a, w_pool_b, scale_b,
           w_gate, w_val, conv_w, conv_b, w_down):
    batch, seq, d = x_prompt.shape
    dec_batch, dec_seq, _ = x_sample.shape
    depth = w_gate.shape[0]
    n_heads = w_s_a.shape[1]
    d_v = w_out_a.shape[1]
    hd = d_v // n_heads
    bf16 = jnp.bfloat16

    n_steps = (batch * seq) // TOKEN_TILE
    gd = w_pool_b.shape[-1]
    w_pool_2d = w_pool_b.reshape(w_pool_b.shape[0], -1, gd)

    def ffn_jobs(layer):
        return [_cast_job(w_gate, layer, 1, FFN_CAST_CHUNKS), _cast_job(w_val, layer, 1, FFN_CAST_CHUNKS),
                _cast_job(w_down, layer, 0, FFN_CAST_CHUNKS)]

    def gmlp_jobs(mixer):
        return [_cast_job(w_in_a, mixer, 1, n_steps), _cast_job(w_out_a, mixer, 0, n_steps)]

    assert FFN_CAST_CHUNKS <= n_steps
    gmlp_w = (w_in_a[0].astype(bf16), w_out_a[0].astype(bf16))

    bias = jnp.repeat(jnp.swapaxes(b_s_a, 1, 2), hd, axis=2)
    ws4 = w_s_a[:, :, :dec_seq, :dec_seq].reshape(-1, n_heads, dec_seq * dec_seq)
    b4 = b_s_a[:, :, :dec_seq]

    xp = x_prompt.reshape(batch * seq, d)
    xs = x_sample
    state_pool_tm = jnp.swapaxes(state_pool, 1, 2)
    v_rows, pool_p, pool_s, conv_p, conv_s = [], [], [], [], []
    for i in range(depth):
        j = i // 2
        nfin = norm_final if i == depth - 1 else None
        if i % 2 == 0:
            xp, *ffn_w = _gmlp_prompt_call(xp, norm_mix, gmlp_w[0], g_v_a, w_s_a, bias, gmlp_w[1],
                                           ffn_jobs(i), layer=i, mixer=j)
            xs, v = _gmlp_sample_call(xs, norm_mix, gmlp_w[0], g_v_a, ws4[j], b4[j], gmlp_w[1],
                                      layer=i, mixer=j)
            v_rows.append(v)
        else:
            pool_w = pool_w.reshape(-1, gd, gd)
            xs, ps = _pool_sample_call(xs, state_pool_tm, norm_mix, pool_w, scale_b, layer=i, mixer=j)
            pool_s.append(ps)
        if i == depth - 1:
            jobs = []
        elif i % 2 == 0:
            jobs = [_cast_job(w_pool_2d, j, 0, n_steps)] + ffn_jobs(i + 1)
        else:
            jobs = gmlp_jobs(j + 1)
        fused_pool = (norm_mix, pool_w, scale_b, j) if i % 2 else None
        xp, cp, *next_w = _ffn_prompt_call(xp, norm_ffn, ffn_w[0], ffn_w[1], conv_w, conv_b, ffn_w[2], nfin,
                                           jobs, layer=i, seq=seq, pool=fused_pool)
        if fused_pool:
            pool_p.append(next_w.pop(0))
        xs, cs = _ffn_sample_call(xs, state_ffn_conv, norm_ffn, ffn_w[0], ffn_w[1], conv_w, conv_b,
                                  ffn_w[2], nfin, layer=i)
        conv_p.append(cp)
        conv_s.append(cs)
        if i % 2 == 0 and i < depth - 1:
            pool_w, ffn_w = next_w[0], next_w[1:]
        elif i < depth - 1:
            gmlp_w = next_w
    return (xp.reshape(batch, seq, d), xs, jnp.stack(v_rows),
            jnp.stack(pool_p), jnp.swapaxes(jnp.stack(pool_s), 1, 2), jnp.stack(conv_p), jnp.stack(conv_s))
```

```python
import functools
import math

import jax
import jax.numpy as jnp
from jax import lax
from jax.experimental import pallas as pl
from jax.experimental.pallas import tpu as pltpu

EPS = 1e-6
CHUNK = 128
POOL_WINDOWS = (2, 4, 8, 16)
POOL_BUF = max(POOL_WINDOWS) - 1
CONV_W = 3
SUBLANES = 8
FFN_TILE = 512
GMLP_TILE = 1024
FF_CHUNK = 512
GMLP_CHUNK = 1024
FFN_CAST_CHUNKS = 11
OUT_CHUNKS = 4
STAGE_BLOCKS = OUT_CHUNKS
VMEM_LIMIT_BYTES = 56 * 1024 * 1024

_GELU_C = math.sqrt(2.0 / math.pi)


def _const_spec(shape):
    zeros = (0,) * len(shape)
    return pl.BlockSpec(shape, lambda i: zeros, pipeline_mode=pl.Buffered(1))


def _layer_spec(stacked_shape, layer):
    tail = (0,) * (len(stacked_shape) - 1)
    return pl.BlockSpec((None,) + tuple(stacked_shape[1:]), lambda i: (layer,) + tail,
                        pipeline_mode=pl.Buffered(1))


def _cast_job(src, layer, axis, n_chunks):
    rows, cols = src.shape[-2:]
    blk = (rows // n_chunks, cols) if axis == 0 else (rows, cols // n_chunks)
    assert blk[0] * (n_chunks if axis == 0 else 1) == rows and blk[1] * (n_chunks if axis == 1 else 1) == cols

    def chunk(i):
        c = jnp.minimum(i, n_chunks - 1)
        return (c, 0) if axis == 0 else (0, c)

    return (src, pl.BlockSpec((None,) + blk, lambda i: (layer,) + chunk(i)),
            pl.BlockSpec(blk, chunk), jax.ShapeDtypeStruct((rows, cols), jnp.bfloat16))


def _run_casts(cast_refs):
    n = len(cast_refs) // 2
    for src_ref, dst_ref in zip(cast_refs[:n], cast_refs[n:]):
        dst_ref[...] = src_ref[...].astype(jnp.bfloat16)


def _rms(x, g):
    ms = jnp.mean(x * x, axis=-1, keepdims=True)
    return x * lax.rsqrt(ms + EPS) * g


def _gelu_tanh(x):
    inner = x * (_GELU_C + (_GELU_C * 0.044715) * (x * x))
    return x * (0.5 + 0.5 * jnp.tanh(inner))


def _silu(x):
    return x * (1.0 / (1.0 + jnp.exp(-x)))


def _compiler_params():
    return pltpu.CompilerParams(dimension_semantics=("arbitrary",),
                                vmem_limit_bytes=VMEM_LIMIT_BYTES)


def _load_slabs(ref3d):
    return jnp.concatenate([ref3d[:, t, :] for t in range(ref3d.shape[1])], axis=0)


def _store_slabs(ref3d, val):
    n_seq = ref3d.shape[0]
    for t in range(ref3d.shape[1]):
        ref3d[:, t, :] = val[t * n_seq:(t + 1) * n_seq]


def _gmlp_tile(x, nm_ref, win_ref, gv_ref, wout_ref, v_ref, m_ref, gate_fn):
    d_v = wout_ref.shape[0]
    h = _rms(x, nm_ref[...]).astype(jnp.bfloat16)
    chunks = [(c0, min(c0 + GMLP_CHUNK, d_v)) for c0 in range(0, d_v, GMLP_CHUNK)]
    ssq = None
    for c0, c1 in chunks:
        zc = _gelu_tanh(jnp.dot(h, win_ref[:, d_v + c0:d_v + c1], preferred_element_type=jnp.float32))
        v_ref[:, c0:c1] = zc
        part = jnp.sum(zc * zc, axis=-1, keepdims=True)
        ssq = part if ssq is None else ssq + part
    rinv = lax.rsqrt(ssq * (1.0 / d_v) + EPS)
    for c0, c1 in chunks:
        uc = _gelu_tanh(jnp.dot(h, win_ref[:, c0:c1], preferred_element_type=jnp.float32))
        vc = v_ref[:, c0:c1] * rinv * gv_ref[:, c0:c1]
        gate_fn(uc, vc, c0, c1)
    y = jnp.dot(m_ref[...], wout_ref[...], preferred_element_type=jnp.float32)
    return x + y


def _gmlp_prompt_kernel(*refs, layer, mixer, n_cast):
    x_ref, nm_ref, win_ref, gv_ref, ws_ref, bias_ref, wout_ref = refs[:7]
    o_ref = refs[7 + n_cast]
    v_ref, m_ref = refs[-2:]
    _run_casts(refs[7:7 + n_cast] + refs[8 + n_cast:8 + 2 * n_cast])
    tm = x_ref.shape[0]
    hd = wout_ref.shape[0] // ws_ref.shape[0]
    row = lax.broadcasted_iota(jnp.int32, (CHUNK, CHUNK), 0)
    col = lax.broadcasted_iota(jnp.int32, (CHUNK, CHUNK), 1)

    def gate(uc, vc, c0, c1):
        vb = vc.astype(jnp.bfloat16)
        for hh in range(c0 // hd, c1 // hd):
            ws = jnp.where(row >= col, ws_ref[hh], 0.0).astype(jnp.bfloat16)
            l0, l1 = hh * hd - c0, (hh + 1) * hd - c0
            for c in range(tm // CHUNK):
                r0, r1 = c * CHUNK, (c + 1) * CHUNK
                s = jnp.dot(ws, vb[r0:r1, l0:l1], preferred_element_type=jnp.float32)
                s = s + bias_ref[:, c0 + l0:c0 + l1]
                m_ref[r0:r1, c0 + l0:c0 + l1] = (uc[r0:r1, l0:l1] * s).astype(jnp.bfloat16)

    o_ref[...] = _gmlp_tile(x_ref[...], nm_ref.at[layer:layer + 1, :], win_ref,
                            gv_ref.at[mixer:mixer + 1, :], wout_ref, v_ref, m_ref, gate)


def _gmlp_sample_kernel(xs_ref, nm_ref, win_ref, gv_ref, ws4_ref, b4_ref, wout_ref, os_ref, vs_ref,
                        v_ref, m_ref, *, layer, mixer):
    n_seq, seq_len, _ = xs_ref.shape
    hd = wout_ref.shape[0] // ws4_ref.shape[0]

    def gate(uc, vc, c0, c1):
        for t in range(seq_len):
            vs_ref[:, t, c0:c1] = vc[t * n_seq:(t + 1) * n_seq]
        for hh in range(c0 // hd, c1 // hd):
            l0, l1 = hh * hd - c0, (hh + 1) * hd - c0
            vj = [vc[j * n_seq:(j + 1) * n_seq, l0:l1] for j in range(seq_len)]
            for t in range(seq_len):
                s = b4_ref[hh, t] + ws4_ref[hh, t * seq_len] * vj[0]
                for j in range(1, t + 1):
                    s = s + ws4_ref[hh, t * seq_len + j] * vj[j]
                r0, r1 = t * n_seq, (t + 1) * n_seq
                m_ref[r0:r1, c0 + l0:c0 + l1] = (uc[r0:r1, l0:l1] * s).astype(jnp.bfloat16)

    out = _gmlp_tile(_load_slabs(xs_ref), nm_ref.at[layer:layer + 1, :], win_ref,
                     gv_ref.at[mixer:mixer + 1, :], wout_ref, v_ref, m_ref, gate)
    _store_slabs(os_ref, out)


def _gmlp_prompt_call(xp, norm_mix, w_in, g_v, w_s, bias, w_out, cast_jobs, *, layer, mixer):
    n, d = xp.shape
    d_v = w_out.shape[0]
    tm = GMLP_TILE
    assert n % tm == 0 and tm % CHUNK == 0
    return pl.pallas_call(
        functools.partial(_gmlp_prompt_kernel, layer=layer, mixer=mixer, n_cast=len(cast_jobs)),
        grid=(n // tm,),
        in_specs=[
            pl.BlockSpec((tm, d), lambda i: (i, 0)),
            _const_spec(norm_mix.shape),
            _const_spec(w_in.shape),
            _const_spec(g_v.shape),
            _layer_spec(w_s.shape, mixer),
            _layer_spec(bias.shape, mixer),
            _const_spec(w_out.shape),
        ] + [job[1] for job in cast_jobs],
        out_specs=[pl.BlockSpec((tm, d), lambda i: (i, 0))] + [job[2] for job in cast_jobs],
        out_shape=[jax.ShapeDtypeStruct((n, d), jnp.float32)] + [job[3] for job in cast_jobs],
        scratch_shapes=[pltpu.VMEM((tm, d_v), jnp.float32), pltpu.VMEM((tm, d_v), jnp.bfloat16)],
        compiler_params=_compiler_params(),
        name="gmlp_mixer",
    )(xp, norm_mix, w_in, g_v, w_s, bias, w_out, *[job[0] for job in cast_jobs])


def _gmlp_sample_call(xs, norm_mix, w_in, g_v, ws4, b4, w_out, *, layer, mixer):
    n_seq, seq_len, d = xs.shape
    d_v = w_out.shape[0]
    assert n_seq % SUBLANES == 0
    smem = pl.BlockSpec(memory_space=pltpu.SMEM)
    v_shape = (n_seq, seq_len, d_v)
    return pl.pallas_call(
        functools.partial(_gmlp_sample_kernel, layer=layer, mixer=mixer),
        grid=(1,),
        in_specs=[
            _const_spec(xs.shape),
            _const_spec(norm_mix.shape),
            _const_spec(w_in.shape),
            _const_spec(g_v.shape),
            smem, smem,
            _const_spec(w_out.shape),
        ],
        out_specs=[pl.BlockSpec(xs.shape, lambda i: (0, 0, 0)), pl.BlockSpec(v_shape, lambda i: (0, 0, 0))],
        out_shape=[jax.ShapeDtypeStruct(xs.shape, jnp.float32), jax.ShapeDtypeStruct(v_shape, jnp.float32)],
        scratch_shapes=[pltpu.VMEM((n_seq * seq_len, d_v), jnp.float32),
                        pltpu.VMEM((n_seq * seq_len, d_v), jnp.bfloat16)],
        compiler_params=_compiler_params(),
        name="gmlp_mixer_sample",
    )(xs, norm_mix, w_in, g_v, ws4, b4, w_out)


def _ffn_tile(h, wg_ref, wv_ref, cw_ref, cb_ref, wd_ref, g_ref, history_fn, emit_fn, n_out_chunks=1):
    d_ff, d = wd_ref.shape
    bounds = list(range(0, d_ff, FF_CHUNK)) + [d_ff]
    for c0, c1 in zip(bounds[:-1], bounds[1:]):
        a = jnp.dot(h, wg_ref[:, c0:c1], preferred_element_type=jnp.float32)
        val = jnp.dot(h, wv_ref[:, c0:c1], preferred_element_type=jnp.float32)
        prev1, prev2 = history_fn(a, c0, c1)
        conv = (cb_ref[:, c0:c1] + cw_ref[0:1, c0:c1] * prev2
                + cw_ref[1:2, c0:c1] * prev1 + cw_ref[2:3, c0:c1] * a)
        g_ref[:, c0:c1] = (_silu(conv) * val).astype(jnp.bfloat16)
    step = d // n_out_chunks
    for k in range(n_out_chunks):
        n0, n1 = k * step, (k + 1) * step
        emit_fn(k, n0, n1, jnp.dot(g_ref[...], wd_ref[:, n0:n1], preferred_element_type=jnp.float32))


def _pool_mix(x, h, prev, first_pos, wp_ref, scale_ref):
    rows = x.shape[0]
    gd = wp_ref.shape[1]
    halo = POOL_BUF + 1
    pos = first_pos + lax.broadcasted_iota(jnp.int32, (rows, gd), 0)
    p_groups = []
    for g, w in enumerate(POOL_WINDOWS):
        c0, c1 = g * gd, (g + 1) * gd
        hg = h[:, c0:c1]
        acc = jnp.concatenate([prev[:, c0:c1], hg], axis=0)
        span = 1
        while span < w:
            acc = acc[span:] + acc[:-span]
            span *= 2
        total = acc[halo - (w - 1):]
        cnt = jnp.minimum(w, pos + 1).astype(jnp.float32)
        p_groups.append(total / cnt - hg)
    return x + _pool_project(p_groups, wp_ref, scale_ref)


def _ffn_prompt_kernel(*refs, layer, tiles_per_seq, final, n_cast, pool_mixer):
    pool = pool_mixer is not None
    it = iter(refs)
    x_ref, xn_ref, nf_ref, wg_ref, wv_ref, cw_ref, cbias_ref, wd_ref = (next(it) for _ in range(8))
    ng_ref = next(it) if final else None
    nm_ref, wp_ref, scale_ref = (next(it), next(it), next(it)) if pool else (None, None, None)
    cast_refs = [next(it) for _ in range(n_cast)]
    o_ref, cp_ref = next(it), next(it)
    pp_ref = next(it) if pool else None
    cast_refs += [next(it) for _ in range(n_cast)]
    g_ref, carry_ref, h_ref = (next(it) for _ in range(3))
    x1_ref, pcarry_ref = (next(it), next(it)) if pool else (None, None)
    i = pl.program_id(0)
    tm = xn_ref.shape[0]
    rb = tm // STAGE_BLOCKS
    keep = CONV_W - 1
    halo = POOL_BUF + 1

    def stage_block(src_ref, r, tile_in_seq):
        r0, r1 = r * rb, (r + 1) * rb
        x = src_ref[r0:r1, :]
        if pool:
            h1 = _rms(x, nm_ref[layer:layer + 1, :])
            prev = pcarry_ref[...]
            pcarry_ref[...] = h1[rb - halo:, :]
            if r == STAGE_BLOCKS - 1:
                pp_ref[0] = h1[rb - POOL_BUF:, :]
            x = _pool_mix(x, h1, prev, tile_in_seq * tm + r0, wp_ref,
                          scale_ref.at[pool_mixer:pool_mixer + 1, :])
            x1_ref[r0:r1, :] = x
        h_ref[r0:r1, :] = _rms(x, nf_ref[layer:layer + 1, :]).astype(jnp.bfloat16)

    @pl.when(i == 0)
    def _first_tile():
        if pool:
            pcarry_ref[...] = jnp.zeros_like(pcarry_ref)
        for r in range(STAGE_BLOCKS):
            stage_block(x_ref, r, 0)

    @pl.when(i % tiles_per_seq == 0)
    def _new_sequence():
        carry_ref[...] = jnp.zeros_like(carry_ref)

    if pool:
        @pl.when((i + 1) % tiles_per_seq == 0)
        def _next_tile_starts_sequence():
            pcarry_ref[...] = jnp.zeros_like(pcarry_ref)

    _run_casts(cast_refs)

    def history(a, c0, c1):
        head = carry_ref[:, c0:c1]
        carry_ref[:, c0:c1] = a[tm - SUBLANES:, :]
        cp_ref[0, :, c0:c1] = a[tm - keep:, :]
        ext = jnp.concatenate([head, a], axis=0)
        return ext[SUBLANES - 1:SUBLANES - 1 + tm], ext[SUBLANES - 2:SUBLANES - 2 + tm]

    def emit(k, n0, n1, y):
        if pool:
            if k == 0:
                o_ref[...] = x1_ref[...]
            o_ref[:, n0:n1] = o_ref[:, n0:n1] + y
        else:
            o_ref[:, n0:n1] = x_ref[:, n0:n1] + y
        stage_block(xn_ref, k, (i + 1) % tiles_per_seq)

    _ffn_tile(h_ref[...], wg_ref, wv_ref, cw_ref, cbias_ref.at[layer:layer + 1, :], wd_ref, g_ref,
              history, emit, OUT_CHUNKS)
    if final:
        o_ref[...] = _rms(o_ref[...], ng_ref[...])


def _ffn_sample_kernel(*refs, layer, final):
    it = iter(refs)
    xs_ref, cs_ref, nf_ref, wg_ref, wv_ref, cw_ref, cbias_ref, wd_ref = (next(it) for _ in range(8))
    ng_ref = next(it) if final else None
    os_ref, cso_ref, g_ref = (next(it) for _ in range(3))
    n_seq, seq_len, _ = xs_ref.shape
    rows = n_seq * seq_len
    keep = CONV_W - 1

    def history(a, c0, c1):
        old = [cs_ref[:, k, c0:c1] for k in range(keep)]
        for k in range(keep):
            t = seq_len - keep + k
            cso_ref[:, k, c0:c1] = a[t * n_seq:(t + 1) * n_seq]
        prev1 = jnp.concatenate([old[1], a[:rows - n_seq]], axis=0)
        prev2 = jnp.concatenate([old[0], old[1], a[:rows - 2 * n_seq]], axis=0)
        return prev1, prev2

    x = _load_slabs(xs_ref)

    def emit(k, n0, n1, y):
        out = x + y
        _store_slabs(os_ref, out if ng_ref is None else _rms(out, ng_ref[...]))

    _ffn_tile(_rms(x, nf_ref[layer:layer + 1, :]).astype(jnp.bfloat16), wg_ref, wv_ref, cw_ref,
              cbias_ref.at[layer:layer + 1, :], wd_ref, g_ref, history, emit)


def _ffn_weight_specs(norm_ffn, wg, wv, cw, cb, wd, norm_final, layer):
    args = [norm_ffn, wg, wv, cw, cb, wd]
    specs = [_const_spec(norm_ffn.shape), _const_spec(wg.shape), _const_spec(wv.shape),
             _layer_spec(cw.shape, layer), _const_spec(cb.shape), _const_spec(wd.shape)]
    if norm_final is not None:
        args.append(norm_final.reshape(1, -1))
        specs.append(_const_spec((1, norm_final.shape[0])))
    return args, specs


def _ffn_prompt_call(xp, norm_ffn, wg, wv, cw, cb, wd, norm_final, cast_jobs, *, layer, seq, pool=None):
    n, d = xp.shape
    d_ff = wd.shape[0]
    tm = FFN_TILE
    assert n % tm == 0 and seq % tm == 0 and (tm // STAGE_BLOCKS) % (2 * SUBLANES) == 0
    assert tm // STAGE_BLOCKS >= POOL_BUF + 1 and d % OUT_CHUNKS == 0
    n_tiles, tiles_per_seq = n // tm, seq // tm
    keep = CONV_W - 1
    w_args, w_specs = _ffn_weight_specs(norm_ffn, wg, wv, cw, cb, wd, norm_final, layer)
    next_tile = pl.BlockSpec((tm, d), lambda i: (jnp.minimum(i + 1, n_tiles - 1), 0))
    in_specs = [_const_spec((tm, d)) if pool else pl.BlockSpec((tm, d), lambda i: (i, 0)), next_tile] + w_specs
    args = [xp, xp] + w_args
    out_specs = [pl.BlockSpec((tm, d), lambda i: (i, 0)),
                 pl.BlockSpec((1, keep, d_ff), lambda i: (i // tiles_per_seq, 0, 0))]
    out_shape = [jax.ShapeDtypeStruct((n, d), jnp.float32),
                 jax.ShapeDtypeStruct((n // seq, keep, d_ff), jnp.float32)]
    scratch = [pltpu.VMEM((tm, d_ff), jnp.bfloat16), pltpu.VMEM((SUBLANES, d_ff), jnp.float32),
               pltpu.VMEM((tm, d), jnp.bfloat16)]
    if pool:
        norm_mix, wp, scale, mixer = pool
        args += [norm_mix, wp, scale]
        in_specs += [_const_spec(norm_mix.shape), _const_spec(wp.shape), _const_spec(scale.shape)]
        out_specs.append(pl.BlockSpec(
            (1, POOL_BUF, d), lambda i: (jnp.minimum(i + 1, n_tiles - 1) // tiles_per_seq, 0, 0)))
        out_shape.append(jax.ShapeDtypeStruct((n // seq, POOL_BUF, d), jnp.float32))
        scratch += [pltpu.VMEM((tm, d), jnp.float32), pltpu.VMEM((POOL_BUF + 1, d), jnp.float32)]
    return pl.pallas_call(
        functools.partial(_ffn_prompt_kernel, layer=layer, tiles_per_seq=tiles_per_seq,
                          final=norm_final is not None, n_cast=len(cast_jobs),
                          pool_mixer=pool[3] if pool else None),
        grid=(n_tiles,),
        in_specs=in_specs + [job[1] for job in cast_jobs],
        out_specs=out_specs + [job[2] for job in cast_jobs],
        out_shape=out_shape + [job[3] for job in cast_jobs],
        scratch_shapes=scratch,
        compiler_params=_compiler_params(),
        name="conv_ffn",
    )(*args, *[job[0] for job in cast_jobs])


def _ffn_sample_call(xs, conv_state, norm_ffn, wg, wv, cw, cb, wd, norm_final, *, layer):
    n_seq, seq_len, d = xs.shape
    d_ff = wd.shape[0]
    keep = CONV_W - 1
    assert seq_len >= keep and n_seq % SUBLANES == 0
    w_args, w_specs = _ffn_weight_specs(norm_ffn, wg, wv, cw, cb, wd, norm_final, layer)
    return pl.pallas_call(
        functools.partial(_ffn_sample_kernel, layer=layer, final=norm_final is not None),
        grid=(1,),
        in_specs=[_const_spec(xs.shape), _layer_spec(conv_state.shape, layer)] + w_specs,
        out_specs=[pl.BlockSpec(xs.shape, lambda i: (0, 0, 0)),
                   pl.BlockSpec((n_seq, keep, d_ff), lambda i: (0, 0, 0))],
        out_shape=[jax.ShapeDtypeStruct(xs.shape, jnp.float32),
                   jax.ShapeDtypeStruct((n_seq, keep, d_ff), jnp.float32)],
        scratch_shapes=[pltpu.VMEM((n_seq * seq_len, d_ff), jnp.bfloat16)],
        compiler_params=_compiler_params(),
        name="conv_ffn_sample",
    )(xs, conv_state, *w_args)


def _pool_project(p_groups, wp_ref, scale_ref):
    gd = wp_ref.shape[1]
    ys = [jnp.dot(p.astype(jnp.bfloat16), wp_ref[g], preferred_element_type=jnp.float32)
          * scale_ref[:, g * gd:(g + 1) * gd] for g, p in enumerate(p_groups)]
    return jnp.concatenate(ys, axis=-1)


def _pool_sample_kernel(xs_ref, ps_ref, nm_ref, wp_ref, scale_ref, os_ref, pso_ref, hs_ref, *, layer, mixer):
    gd = wp_ref.shape[1]
    n_seq, seq_len, _ = xs_ref.shape
    for t in range(seq_len):
        ht = _rms(xs_ref[:, t, :], nm_ref[layer:layer + 1, :])
        hs_ref[t] = ht
        pso_ref[POOL_BUF - seq_len + t] = ht
    for k in range(POOL_BUF - seq_len):
        pso_ref[k] = ps_ref[k + seq_len]

    def slab(k, c0, c1):
        return ps_ref[k, :, c0:c1] if k < POOL_BUF else hs_ref[k - POOL_BUF, :, c0:c1]

    for t in range(seq_len):
        last = POOL_BUF + t
        p_groups = []
        for g, w in enumerate(POOL_WINDOWS):
            c0, c1 = g * gd, (g + 1) * gd
            total = slab(last, c0, c1)
            for k in range(last - w + 1, last):
                total = total + slab(k, c0, c1)
            p_groups.append(total / float(w) - slab(last, c0, c1))
        os_ref[:, t, :] = xs_ref[:, t, :] + _pool_project(p_groups, wp_ref, scale_ref.at[mixer:mixer + 1, :])


def _pool_sample_call(xs, pool_state_tm, norm_mix, wp, scale, *, layer, mixer):
    n_seq, seq_len, d = xs.shape
    assert seq_len <= POOL_BUF and n_seq % SUBLANES == 0
    return pl.pallas_call(
        functools.partial(_pool_sample_kernel, layer=layer, mixer=mixer),
        grid=(1,),
        in_specs=[
            _const_spec(xs.shape),
            _layer_spec(pool_state_tm.shape, mixer),
            _const_spec(norm_mix.shape),
            _const_spec(wp.shape),
            _const_spec(scale.shape),
        ],
        out_specs=[pl.BlockSpec(xs.shape, lambda i: (0, 0, 0)),
                   pl.BlockSpec((POOL_BUF, n_seq, d), lambda i: (0, 0, 0))],
        out_shape=[jax.ShapeDtypeStruct(xs.shape, jnp.float32),
                   jax.ShapeDtypeStruct((POOL_BUF, n_seq, d), jnp.float32)],
        scratch_shapes=[pltpu.VMEM((seq_len, n_seq, d), jnp.float32)],
        compiler_params=_compiler_params(),
        name="pool_mixer_sample",
    )(xs, pool_state_tm, norm_mix, wp, scale)


def kernel(x_prompt, x_sample, state_pool, state_ffn_conv, norm_mix, norm_ffn, norm_final,
           w_in_a, g_v_a, w_s_a, b_s_a, w_out_a, w_pool_b, scale_b,
           w_gate, w_val, conv_w, conv_b, w_down):
    batch, seq, d = x_prompt.shape
    dec_batch, dec_seq, _ = x_sample.shape
    depth = w_gate.shape[0]
    n_heads = w_s_a.shape[1]
    d_v = w_out_a.shape[1]
    hd = d_v // n_heads
    bf16 = jnp.bfloat16

    ffn_steps = (batch * seq) // FFN_TILE
    gd = w_pool_b.shape[-1]
    w_pool_2d = w_pool_b.reshape(w_pool_b.shape[0], -1, gd)

    def ffn_jobs(layer):
        return [_cast_job(w_gate, layer, 1, FFN_CAST_CHUNKS), _cast_job(w_val, layer, 1, FFN_CAST_CHUNKS),
                _cast_job(w_down, layer, 0, FFN_CAST_CHUNKS)]

    def gmlp_jobs(mixer):
        return [_cast_job(w_in_a, mixer, 1, ffn_steps), _cast_job(w_out_a, mixer, 0, ffn_steps)]

    assert FFN_CAST_CHUNKS <= min(ffn_steps, (batch * seq) // GMLP_TILE)
    gmlp_w = (w_in_a[0].astype(bf16), w_out_a[0].astype(bf16))

    bias = jnp.repeat(jnp.swapaxes(b_s_a, 1, 2), hd, axis=2)
    ws4 = w_s_a[:, :, :dec_seq, :dec_seq].reshape(-1, n_heads, dec_seq * dec_seq)
    b4 = b_s_a[:, :, :dec_seq]

    xp = x_prompt.reshape(batch * seq, d)
    xs = x_sample
    state_pool_tm = jnp.swapaxes(state_pool, 1, 2)
    v_rows, pool_p, pool_s, conv_p, conv_s = [], [], [], [], []
    for i in range(depth):
        j = i // 2
        nfin = norm_final if i == depth - 1 else None
        if i % 2 == 0:
            xp, *ffn_w = _gmlp_prompt_call(xp, norm_mix, gmlp_w[0], g_v_a, w_s_a, bias, gmlp_w[1],
                                           ffn_jobs(i), layer=i, mixer=j)
            xs, v = _gmlp_sample_call(xs, norm_mix, gmlp_w[0], g_v_a, ws4[j], b4[j], gmlp_w[1],
                                      layer=i, mixer=j)
            v_rows.append(v)
        else:
            pool_w = pool_w.reshape(-1, gd, gd)
            xs, ps = _pool_sample_call(xs, state_pool_tm, norm_mix, pool_w, scale_b, layer=i, mixer=j)
            pool_s.append(ps)
        if i == depth - 1:
            jobs = []
        elif i % 2 == 0:
            jobs = [_cast_job(w_pool_2d, j, 0, ffn_steps)] + ffn_jobs(i + 1)
        else:
            jobs = gmlp_jobs(j + 1)
        fused_pool = (norm_mix, pool_w, scale_b, j) if i % 2 else None
        xp, cp, *next_w = _ffn_prompt_call(xp, norm_ffn, ffn_w[0], ffn_w[1], conv_w, conv_b, ffn_w[2], nfin,
                                           jobs, layer=i, seq=seq, pool=fused_pool)
        if fused_pool:
            pool_p.append(next_w.pop(0))
        xs, cs = _ffn_sample_call(xs, state_ffn_conv, norm_ffn, ffn_w[0], ffn_w[1], conv_w, conv_b,
                                  ffn_w[2], nfin, layer=i)
        conv_p.append(cp)
        conv_s.append(cs)
        if i % 2 == 0 and i < depth - 1:
            pool_w, ffn_w = next_w[0], next_w[1:]
        elif i < depth - 1:
            gmlp_w = next_w
    return (xp.reshape(batch, seq, d), xs, jnp.stack(v_rows),
            jnp.stack(pool_p), jnp.swapaxes(jnp.stack(pool_s), 1, 2), jnp.stack(conv_p), jnp.stack(conv_s))
```
